```python
import math
import jax, jax.numpy as jnp
from jax import lax
import numpy as np

D_MODEL = 1024
BATCH = 2
SEQ = 16384
DEPTH = 4

N_MIXERS = 3
N_A = (DEPTH + 2) // 3
N_B = (DEPTH + 1) // 3
N_C = DEPTH // 3

DEEPNORM_ALPHA = (2.0 * DEPTH) ** 0.25
DEEPNORM_BETA = (8.0 * DEPTH) ** -0.25
LN_EPS = 1e-5
RMS_EPS = 1e-6

D_FF = 2816

GMLP_CHUNK = 128
GMLP_WIDTH = D_MODEL
GMLP_GROUPS = 8
GMLP_GROUP_DIM = GMLP_WIDTH // GMLP_GROUPS

HGRN_EXPAND = 128
HGRN_HEADS = D_MODEL // HGRN_EXPAND
HGRN_DK = HGRN_EXPAND
HGRN_DV = D_MODEL // HGRN_HEADS
HGRN_WIDTH = HGRN_HEADS * HGRN_DK
HGRN_CHUNK = 64

ATTN_WINDOW = 128
ATTN_BLOCK = 128
ATTN_HEAD_DIM = 64
ATTN_Q_HEADS = D_MODEL // ATTN_HEAD_DIM
ATTN_KV_HEADS = 2
ATTN_GROUP = ATTN_Q_HEADS // ATTN_KV_HEADS
ATTN_QKV_WIDTH = (ATTN_Q_HEADS + 2 * ATTN_KV_HEADS) * ATTN_HEAD_DIM

kernel_name = 'hybrid_gmlp_hgrn2_swa_sink_deepnorm_macaron'


def layer_norm(x, g, b):
    xf = x.astype(jnp.float32)
    mu = jnp.mean(xf, axis=-1, keepdims=True)
    var = jnp.mean(jnp.square(xf - mu), axis=-1, keepdims=True)
    y = (xf - mu) * lax.rsqrt(var + LN_EPS)
    return (y * g.astype(jnp.float32) + b.astype(jnp.float32)).astype(x.dtype)


def rms_norm(x, g):
    xf = x.astype(jnp.float32)
    y = xf * lax.rsqrt(jnp.mean(jnp.square(xf), axis=-1, keepdims=True) + RMS_EPS)
    return y * g.astype(jnp.float32)


def swiglu(x, w_gate_up, w_down):
    gate, up = jnp.split(x @ w_gate_up, 2, axis=-1)
    return (jax.nn.silu(gate) * up) @ w_down


def chunked_gmlp(x, w_in, ln_g, ln_b, w_s, b_s, w_out):
    B, S, _ = x.shape
    nc = S // GMLP_CHUNK
    z = jax.nn.gelu(x @ w_in, approximate=False)
    u, v = jnp.split(z, 2, axis=-1)
    v = layer_norm(v, ln_g, ln_b)
    v = v.reshape(B, nc, GMLP_CHUNK, GMLP_GROUPS, GMLP_GROUP_DIM)
    causal = jnp.tril(jnp.ones((GMLP_CHUNK, GMLP_CHUNK), dtype=bool))
    w = jnp.where(causal[None], w_s, jnp.zeros_like(w_s))
    mixed = jnp.einsum('gts,bcsgd->bctgd', w, v) + jnp.transpose(b_s)[:, :, None]
    return (u * mixed.reshape(B, S, GMLP_WIDTH)) @ w_out


def hgrn2(x, w_in, lower_bound, norm_g, w_out):
    B, S, _ = x.shape
    H, K, V, C = HGRN_HEADS, HGRN_DK, HGRN_DV, HGRN_CHUNK
    nc = S // C
    q, f, i, g = jnp.split(x @ w_in, 4, axis=-1)
    q = jax.nn.silu(q.astype(jnp.float32))
    f = lower_bound + (1.0 - lower_bound) * jax.nn.sigmoid(f.astype(jnp.float32))
    k = 1.0 - f
    log_f = jnp.log(f)

    def to_chunks(t, d):
        return t.reshape(B, nc, C, H, d).transpose(1, 0, 3, 2, 4)

    qc = to_chunks(q, K)
    kc = to_chunks(k, K)
    vc = to_chunks(i.astype(jnp.float32), V)
    gcum = jnp.cumsum(to_chunks(log_f, K), axis=-2)
    mask = jnp.tril(jnp.ones((C, C), dtype=bool))[:, :, None]

    def step(state, inp):
        q_, k_, v_, gc_ = inp
        diff = gc_[..., :, None, :] - gc_[..., None, :, :]
        decay = jnp.exp(jnp.where(mask, diff, -jnp.inf))
        scores = jnp.einsum('bhtk,bhtsk,bhsk->bhts', q_, decay, k_)
        o = jnp.einsum('bhts,bhsv->bhtv', scores, v_) + jnp.einsum('bhtk,bhkv->bhtv', q_ * jnp.exp(gc_), state)
        g_last = gc_[..., -1:, :]
        k_dec = k_ * jnp.exp(g_last - gc_)
        state = jnp.exp(g_last[..., 0, :])[..., None] * state + jnp.einsum('bhsk,bhsv->bhkv', k_dec, v_)
        return state, o

    state0 = jnp.zeros((B, H, K, V), jnp.float32)
    _, o = lax.scan(step, state0, (qc, kc, vc, gcum))
    o = o.transpose(1, 0, 3, 2, 4).reshape(B, S, H, V)
    o = rms_norm(o, norm_g).reshape(B, S, H * V)
    o = (o * jax.nn.silu(g.astype(jnp.float32))).astype(x.dtype)
    return o @ w_out


def sliding_window_attention(x, w_qkv, b_qkv, sinks, w_o, b_o):
    B, S, _ = x.shape
    nb = S // ATTN_BLOCK
    L, HD, HKV, G = ATTN_BLOCK, ATTN_HEAD_DIM, ATTN_KV_HEADS, ATTN_GROUP
    qkv = x @ w_qkv + b_qkv
    q, k, v = jnp.split(qkv, [ATTN_Q_HEADS * HD, ATTN_Q_HEADS * HD + HKV * HD], axis=-1)
    q = q.reshape(B, nb, L, HKV, G, HD) * (HD ** -0.5)
    k = k.reshape(B, nb, L, HKV, HD)
    v = v.reshape(B, nb, L, HKV, HD)

    def with_prev(t):
        prev = jnp.concatenate([jnp.zeros_like(t[:, :1]), t[:, :-1]], axis=1)
        return jnp.concatenate([prev, t], axis=2)

    k2, v2 = with_prev(k), with_prev(v)
    scores = jnp.einsum('bnqhgd,bnkhd->bnhgqk', q, k2).astype(jnp.float32)
    q_pos = jnp.arange(L)[:, None] + L
    k_pos = jnp.arange(2 * L)[None, :]
    rel = q_pos - k_pos
    in_band = (rel >= 0) & (rel < ATTN_WINDOW)
    real_key = (jnp.arange(nb)[:, None, None] > 0) | (k_pos[None] >= L)
    mask = in_band[None] & real_key
    scores = jnp.where(mask[None, :, None, None], scores, -jnp.inf)
    sink = jnp.broadcast_to(sinks.astype(jnp.float32).reshape(1, 1, HKV, G, 1, 1), scores.shape[:-1] + (1,))
    probs = jax.nn.softmax(jnp.concatenate([scores, sink], axis=-1), axis=-1)[..., :-1]
    out = jnp.einsum('bnhgqk,bnkhd->bnqhgd', probs.astype(v2.dtype), v2)
    return out.reshape(B, S, ATTN_Q_HEADS * HD) @ w_o + b_o


def setup_inputs(seed: int = 0) -> dict:
    key = jax.random.key(seed)
    ks = jax.random.split(key, 24)

    def nrm(k, shape, scale):
        return jax.random.normal(k, shape, jnp.float32) * scale

    beta = DEEPNORM_BETA
    return {
        'x': nrm(ks[0], (BATCH, SEQ, D_MODEL), 1.0),
        'ffn_w_gate_up': nrm(ks[1], (DEPTH, 2, D_MODEL, 2 * D_FF), D_MODEL ** -0.5),
        'ffn_w_down': nrm(ks[2], (DEPTH, 2, D_FF, D_MODEL), beta * D_FF ** -0.5),
        'ln_gain': 1.0 + nrm(ks[3], (DEPTH, 3, D_MODEL), 0.01),
        'ln_bias': nrm(ks[4], (DEPTH, 3, D_MODEL), 0.01),
        'gmlp_w_in': nrm(ks[5], (N_A, D_MODEL, 2 * GMLP_WIDTH), D_MODEL ** -0.5),
        'gmlp_ln_gain': 1.0 + nrm(ks[6], (N_A, GMLP_WIDTH), 0.01),
        'gmlp_ln_bias': nrm(ks[7], (N_A, GMLP_WIDTH), 0.01),
        'gmlp_w_spatial': nrm(ks[8], (N_A, GMLP_GROUPS, GMLP_CHUNK, GMLP_CHUNK), GMLP_CHUNK ** -0.5),
        'gmlp_b_spatial': 1.0 + nrm(ks[9], (N_A, GMLP_GROUPS, GMLP_CHUNK), 0.01),
        'gmlp_w_out': nrm(ks[10], (N_A, GMLP_WIDTH, D_MODEL), beta * GMLP_WIDTH ** -0.5),
        'hgrn_w_in': nrm(ks[11], (N_B, D_MODEL, 4 * HGRN_WIDTH), D_MODEL ** -0.5),
        'hgrn_lb_logits': nrm(ks[12], (DEPTH, HGRN_WIDTH), 0.1),
        'hgrn_norm_gain': 1.0 + nrm(ks[13], (N_B, HGRN_DV), 0.01),
        'hgrn_w_out': nrm(ks[14], (N_B, HGRN_HEADS * HGRN_DV, D_MODEL), beta * (HGRN_HEADS * HGRN_DV) ** -0.5),
        'attn_w_qkv': nrm(ks[15], (N_C, D_MODEL, ATTN_QKV_WIDTH), D_MODEL ** -0.5),
        'attn_b_qkv': nrm(ks[16], (N_C, ATTN_QKV_WIDTH), 0.01),
        'attn_sinks': nrm(ks[17], (N_C, ATTN_Q_HEADS), 1.0),
        'attn_w_o': nrm(ks[18], (N_C, ATTN_Q_HEADS * ATTN_HEAD_DIM, D_MODEL), beta * (ATTN_Q_HEADS * ATTN_HEAD_DIM) ** -0.5),
        'attn_b_o': nrm(ks[19], (N_C, D_MODEL), 0.01),
    }


def reference(x, ffn_w_gate_up, ffn_w_down, ln_gain, ln_bias,
              gmlp_w_in, gmlp_ln_gain, gmlp_ln_bias, gmlp_w_spatial, gmlp_b_spatial, gmlp_w_out,
              hgrn_w_in, hgrn_lb_logits, hgrn_norm_gain, hgrn_w_out,
              attn_w_qkv, attn_b_qkv, attn_sinks, attn_w_o, attn_b_o):
    lb_cum = jnp.cumsum(jax.nn.softmax(hgrn_lb_logits.astype(jnp.float32), axis=0), axis=0)
    lower_bounds = lb_cum - lb_cum[:1]
    alpha = DEEPNORM_ALPHA
    h = x
    for layer in range(DEPTH):
        kind = layer % N_MIXERS
        slot = layer // N_MIXERS
        h = layer_norm(alpha * h + 0.5 * swiglu(h, ffn_w_gate_up[layer, 0], ffn_w_down[layer, 0]),
                       ln_gain[layer, 0], ln_bias[layer, 0])
        if kind == 0:
            mix = chunked_gmlp(h, gmlp_w_in[slot], gmlp_ln_gain[slot], gmlp_ln_bias[slot],
                               gmlp_w_spatial[slot], gmlp_b_spatial[slot], gmlp_w_out[slot])
        elif kind == 1:
            mix = hgrn2(h, hgrn_w_in[slot], lower_bounds[layer], hgrn_norm_gain[slot], hgrn_w_out[slot])
        else:
            mix = sliding_window_attention(h, attn_w_qkv[slot], attn_b_qkv[slot], attn_sinks[slot],
                                           attn_w_o[slot], attn_b_o[slot])
        h = layer_norm(alpha * h + mix, ln_gain[layer, 1], ln_bias[layer, 1])
        h = layer_norm(alpha * h + 0.5 * swiglu(h, ffn_w_gate_up[layer, 1], ffn_w_down[layer, 1]),
                       ln_gain[layer, 2], ln_bias[layer, 2])
    return h
```

```python
import functools
import math

import jax
import jax.numpy as jnp
from jax import lax
from jax.experimental import pallas as pl
from jax.experimental.pallas import tpu as pltpu

F32 = jnp.float32
BF16 = jnp.bfloat16

LN_EPS = 1e-5
RMS_EPS = 1e-6

LANES = 128
VMEM_LIMIT_BYTES = 56 * 1024 * 1024

TOKEN_TILE = 512
FFN_COL_TILE = 256

GMLP_CHUNK = 128
GMLP_GROUPS = 8
HGRN_HEADS = 8
HGRN_CHUNK = 128
ATTN_BLOCK = 128
ATTN_HEAD_DIM = 64
ATTN_Q_HEADS = 16
ATTN_KV_HEADS = 2


def _dot(a, b):
    return jnp.dot(a, b, preferred_element_type=F32)


def _dot_nt(a, b):
    return lax.dot_general(a, b, (((1,), (1,)), ((), ())), preferred_element_type=F32)


def _dot_tn(a, b):
    return lax.dot_general(a, b, (((0,), (0,)), ((), ())), preferred_element_type=F32)


def _layer_norm(y, gain, bias):
    mu = jnp.mean(y, axis=-1, keepdims=True)
    yc = y - mu
    var = jnp.mean(yc * yc, axis=-1, keepdims=True)
    return yc * lax.rsqrt(var + LN_EPS) * gain + bias


def _silu(x):
    return x * jax.nn.sigmoid(x)


def _gelu_erf(x):
    return 0.5 * x * (1.0 + lax.erf(x * math.sqrt(0.5)))


def _resident(block_shape, index_map):
    return pl.BlockSpec(block_shape, index_map, pipeline_mode=pl.Buffered(1))


def _compiler_params(n_grid_axes):
    return pltpu.CompilerParams(
        dimension_semantics=("arbitrary",) * n_grid_axes,
        vmem_limit_bytes=VMEM_LIMIT_BYTES,
    )


def _ffn_kernel(x_ref, wgu_ref, wd_ref, gain_ref, bias_ref, o_ref, act_ref, *, alpha):
    x = x_ref[...]
    xb = x.astype(BF16)
    d_ff = wd_ref.shape[0]
    for c in range(d_ff // FFN_COL_TILE):
        lo = c * FFN_COL_TILE
        gate = _dot(xb, wgu_ref[:, lo:lo + FFN_COL_TILE])
        up = _dot(xb, wgu_ref[:, d_ff + lo:d_ff + lo + FFN_COL_TILE])
        act_ref[:, lo:lo + FFN_COL_TILE] = (_silu(gate) * up).astype(BF16)
    ffn = _dot(act_ref[...], wd_ref[...])
    o_ref[...] = _layer_norm(alpha * x + 0.5 * ffn, gain_ref[...], bias_ref[...])


def _ffn_layer(h, wgu, wd, gain, bias, layer, half, alpha):
    n, d = h.shape
    d_ff = wd.shape[2]
    tm = min(TOKEN_TILE, n)
    return pl.pallas_call(
        functools.partial(_ffn_kernel, alpha=alpha),
        grid=(n // tm,),
        in_specs=[
            pl.BlockSpec((tm, d), lambda i: (i, 0)),
            _resident((None, None, d, 2 * d_ff), lambda i: (layer, half, 0, 0)),
            _resident((None, None, d_ff, d), lambda i: (layer, half, 0, 0)),
            _resident((1, d), lambda i: (0, 0)),
            _resident((1, d), lambda i: (0, 0)),
        ],
        out_specs=pl.BlockSpec((tm, d), lambda i: (i, 0)),
        out_shape=jax.ShapeDtypeStruct((n, d), F32),
        scratch_shapes=[pltpu.VMEM((tm, d_ff), BF16)],
        compiler_params=_compiler_params(1),
        name=f"ffn_l{layer}_h{half}",
    )(h, wgu, wd, gain, bias)


def _gmlp_kernel(x_ref, win_ref, vg_ref, vb_ref, ws_ref, bs_ref, wout_ref, gain_ref, bias_ref,
                 o_ref, gated_ref, *, alpha):
    x = x_ref[...]
    xb = x.astype(BF16)
    tm = x.shape[0]
    width = wout_ref.shape[0]
    n_chunks = tm // GMLP_CHUNK
    gd = width // GMLP_GROUPS

    v = _gelu_erf(_dot(xb, win_ref[:, width:]))
    v = _layer_norm(v, vg_ref[...], vb_ref[...]).astype(BF16)

    t_idx = lax.broadcasted_iota(jnp.int32, (GMLP_CHUNK, GMLP_CHUNK), 0)
    s_idx = lax.broadcasted_iota(jnp.int32, (GMLP_CHUNK, GMLP_CHUNK), 1)
    causal = s_idx <= t_idx
    for g in range(GMLP_GROUPS):
        w_g = jnp.where(causal, ws_ref[g], 0.0).astype(BF16)
        v_g = jnp.concatenate(
            [v[c * GMLP_CHUNK:(c + 1) * GMLP_CHUNK, g * gd:(g + 1) * gd] for c in range(n_chunks)],
            axis=1)
        mixed = _dot(w_g, v_g)
        b_g = bs_ref[:, g * gd:(g + 1) * gd]
        for c in range(n_chunks):
            gated_ref[c * GMLP_CHUNK:(c + 1) * GMLP_CHUNK, g * gd:(g + 1) * gd] = (
                mixed[:, c * gd:(c + 1) * gd] + b_g)

    u = _gelu_erf(_dot(xb, win_ref[:, :width]))
    mix = _dot((u * gated_ref[...]).astype(BF16), wout_ref[...])
    o_ref[...] = _layer_norm(alpha * x + mix, gain_ref[...], bias_ref[...])


def _gmlp_layer(h, win, vg, vb, ws, bs_full, wout, gain, bias, slot, alpha):
    n, d = h.shape
    width = wout.shape[1]
    tm = min(TOKEN_TILE, n)
    return pl.pallas_call(
        functools.partial(_gmlp_kernel, alpha=alpha),
        grid=(n // tm,),
        in_specs=[
            pl.BlockSpec((tm, d), lambda i: (i, 0)),
            _resident((None, d, 2 * width), lambda i: (slot, 0, 0)),
            _resident((1, width), lambda i: (0, 0)),
            _resident((1, width), lambda i: (0, 0)),
            _resident((None, GMLP_GROUPS, GMLP_CHUNK, GMLP_CHUNK), lambda i: (slot, 0, 0, 0)),
            _resident((GMLP_CHUNK, width), lambda i: (0, 0)),
            _resident((None, width, d), lambda i: (slot, 0, 0)),
            _resident((1, d), lambda i: (0, 0)),
            _resident((1, d), lambda i: (0, 0)),
        ],
        out_specs=pl.BlockSpec((tm, d), lambda i: (i, 0)),
        out_shape=jax.ShapeDtypeStruct((n, d), F32),
        scratch_shapes=[pltpu.VMEM((tm, width), F32)],
        compiler_params=_compiler_params(1),
        name=f"gmlp_s{slot}",
    )(h, win, vg, vb, ws, bs_full, wout, gain, bias)


def _split3_bf16(x):
    hi = x.astype(BF16)
    r1 = x - hi.astype(F32)
    mid = r1.astype(BF16)
    lo = (r1 - mid.astype(F32)).astype(BF16)
    return hi, mid, lo


def _hgrn_kernel(x_ref, win_ref, lbl_ref, ng_ref, wout_ref, gain_ref, bias_ref, o_ref,
                 q_ref, f_ref, i_ref, g_ref, oall_ref, state_ref, *, alpha, layer):
    @pl.when(pl.program_id(1) == 0)
    def _():
        state_ref[...] = jnp.zeros_like(state_ref)

    x = x_ref[...]
    xb = x.astype(BF16)
    tm = x.shape[0]
    width = wout_ref.shape[0]
    dk = width // HGRN_HEADS

    logits = lbl_ref[...]
    e = jnp.exp(logits - jnp.max(logits, axis=0, keepdims=True))
    probs = e / jnp.sum(e, axis=0, keepdims=True)
    lb = jnp.zeros((1, width), F32)
    for j in range(1, layer + 1):
        lb = lb + probs[j:j + 1, :]

    q_ref[...] = _silu(_dot(xb, win_ref[:, 0:width]))
    f_ref[...] = lb + (1.0 - lb) * jax.nn.sigmoid(_dot(xb, win_ref[:, width:2 * width]))
    i_ref[...] = _dot(xb, win_ref[:, 2 * width:3 * width])
    g_ref[...] = _silu(_dot(xb, win_ref[:, 3 * width:4 * width]))

    row = lax.broadcasted_iota(jnp.int32, (HGRN_CHUNK, HGRN_CHUNK), 0)
    col = lax.broadcasted_iota(jnp.int32, (HGRN_CHUNK, HGRN_CHUNK), 1)
    lower = col <= row
    tri = jnp.where(lower, 1.0, 0.0).astype(BF16)
    tri3 = jnp.concatenate([tri, tri, tri], axis=1)
    norm_g = ng_ref[...]

    def chunk_step(c, carry):
        rows = pl.ds(pl.multiple_of(c * HGRN_CHUNK, HGRN_CHUNK), HGRN_CHUNK)
        f = f_ref[rows, :]
        gc = _dot(tri3, jnp.concatenate(_split3_bf16(jnp.log(f)), axis=0))
        g_last = gc[HGRN_CHUNK - 1:HGRN_CHUNK, :]
        g_mid = 0.5 * g_last
        e_mid = jnp.exp(g_mid)
        q_t = q_ref[rows, :] * jnp.exp(gc - g_mid)
        k_t = (1.0 - f) * jnp.exp(g_mid - gc)
        q_d = (q_t * e_mid).astype(BF16)
        k_d = (k_t * e_mid).astype(BF16)
        q_t = q_t.astype(BF16)
        k_t = k_t.astype(BF16)
        v = i_ref[rows, :].astype(BF16)
        decay = jnp.exp(g_last)
        outs = []
        for h in range(HGRN_HEADS):
            hs = slice(h * dk, (h + 1) * dk)
            scores = jnp.where(lower, _dot_nt(q_t[:, hs], k_t[:, hs]), 0.0)
            st = state_ref[h]
            o_h = _dot(scores.astype(BF16), v[:, hs]) + _dot_nt(q_d[:, hs], st.astype(BF16))
            state_ref[h] = st * decay[:, hs] + _dot_tn(v[:, hs], k_d[:, hs])
            ms = jnp.mean(o_h * o_h, axis=-1, keepdims=True)
            outs.append(o_h * lax.rsqrt(ms + RMS_EPS) * norm_g)
        o = jnp.concatenate(outs, axis=1) * g_ref[rows, :]
        oall_ref[rows, :] = o.astype(BF16)
        return carry

    lax.fori_loop(0, tm // HGRN_CHUNK, chunk_step, 0)
    mix = _dot(oall_ref[...], wout_ref[...])
    o_ref[...] = _layer_norm(alpha * x + mix, gain_ref[...], bias_ref[...])


def _hgrn_layer(h, batch, win, lb_logits, norm_g, wout, gain, bias, slot, layer, alpha):
    n, d = h.shape
    width = wout.shape[1]
    seq = n // batch
    tm = min(TOKEN_TILE, seq)
    spb = seq // tm
    depth = lb_logits.shape[0]
    return pl.pallas_call(
        functools.partial(_hgrn_kernel, alpha=alpha, layer=layer),
        grid=(batch, spb),
        in_specs=[
            pl.BlockSpec((tm, d), lambda b, s: (b * spb + s, 0)),
            _resident((None, d, 4 * width), lambda b, s: (slot, 0, 0)),
            _resident((depth, width), lambda b, s: (0, 0)),
            _resident((1, width // HGRN_HEADS), lambda b, s: (0, 0)),
            _resident((None, width, d), lambda b, s: (slot, 0, 0)),
            _resident((1, d), lambda b, s: (0, 0)),
            _resident((1, d), lambda b, s: (0, 0)),
        ],
        out_specs=pl.BlockSpec((tm, d), lambda b, s: (b * spb + s, 0)),
        out_shape=jax.ShapeDtypeStruct((n, d), F32),
        scratch_shapes=[
            pltpu.VMEM((tm, width), F32),
            pltpu.VMEM((tm, width), F32),
            pltpu.VMEM((tm, width), F32),
            pltpu.VMEM((tm, width), F32),
            pltpu.VMEM((tm, width), BF16),
            pltpu.VMEM((HGRN_HEADS, width // HGRN_HEADS, width // HGRN_HEADS), F32),
        ],
        compiler_params=_compiler_params(2),
        name=f"hgrn_s{slot}",
    )(h, win, lb_logits, norm_g, wout, gain, bias)


def _swa_kernel(x_ref, wqkv_ref, bqkv_ref, sink_ref, wo_ref, bo_ref, gain_ref, bias_ref, o_ref,
                q_ref, kv_ref, attn_ref, *, alpha):
    L = ATTN_BLOCK
    hd = ATTN_HEAD_DIM
    group = ATTN_Q_HEADS // ATTN_KV_HEADS
    pairs = group // 2
    kvw = ATTN_KV_HEADS * hd
    first_tile = pl.program_id(1) == 0

    @pl.when(first_tile)
    def _():
        kv_ref[0:L, :] = jnp.zeros((L, 2 * kvw), F32)

    x = x_ref[...]
    xb = x.astype(BF16)
    tm = x.shape[0]
    dq = ATTN_Q_HEADS * hd
    q_ref[...] = ((_dot(xb, wqkv_ref[:, 0:dq]) + bqkv_ref[:, 0:dq]) * (hd ** -0.5)).astype(BF16)
    kv_ref[L:L + tm, :] = _dot(xb, wqkv_ref[:, dq:dq + 2 * kvw]) + bqkv_ref[:, dq:dq + 2 * kvw]

    lane = lax.broadcasted_iota(jnp.int32, (2 * L, kvw), 1)
    low = lane < hd
    qi = lax.broadcasted_iota(jnp.int32, (L, 2 * L), 0)
    kj = lax.broadcasted_iota(jnp.int32, (L, 2 * L), 1)
    in_band = (kj > qi) & (kj <= qi + L)
    band_bias = jnp.where(in_band, 0.0, -jnp.inf)
    first_bias = jnp.where(in_band & (kj >= L), 0.0, -jnp.inf)
    ones_cols = jnp.where(lax.broadcasted_iota(jnp.int32, (2 * L, kvw), 1) < hd, 1.0, 0.0)
    lane_q = lax.broadcasted_iota(jnp.int32, (L, 2 * hd), 1)
    low_q = lane_q < hd

    for n in range(tm // L):
        if n == 0:
            bias = jnp.where(first_tile, first_bias, band_bias)
        else:
            bias = band_bias
        kv2 = kv_ref[n * L:(n + 2) * L, :]
        k2 = kv2[:, 0:kvw]
        v2 = kv2[:, kvw:2 * kvw]
        k2r = pltpu.roll(k2, hd, axis=1)
        v2r = pltpu.roll(v2, hd, axis=1)
        qb = q_ref[n * L:(n + 1) * L, :]
        blocks = []
        for hk in range(ATTN_KV_HEADS):
            k_lo, k_hi = (k2, k2r) if hk == 0 else (k2r, k2)
            v_lo, v_hi = (v2, v2r) if hk == 0 else (v2r, v2)
            keys = jnp.concatenate([jnp.where(low, k_lo, 0.0), jnp.where(low, 0.0, k_hi)],
                                   axis=0).astype(BF16)
            vals = jnp.concatenate(
                [jnp.concatenate([jnp.where(low, v_lo, 0.0), ones_cols], axis=1),
                 jnp.concatenate([jnp.where(low, 0.0, v_hi), 1.0 - ones_cols], axis=1)],
                axis=0).astype(BF16)
            q_pairs = jnp.concatenate(
                [qb[:, (hk * pairs + p) * 2 * hd:(hk * pairs + p + 1) * 2 * hd] for p in range(pairs)],
                axis=0)
            s_all = _dot_nt(q_pairs, keys)
            for p in range(pairs):
                head = hk * group + 2 * p
                sink_e = sink_ref[head]
                sink_o = sink_ref[head + 1]
                s_e = s_all[p * L:(p + 1) * L, 0:2 * L] + bias
                s_o = s_all[p * L:(p + 1) * L, 2 * L:4 * L] + bias
                m_e = jnp.maximum(jnp.max(s_e, axis=-1, keepdims=True), sink_e)
                m_o = jnp.maximum(jnp.max(s_o, axis=-1, keepdims=True), sink_o)
                p_cat = jnp.concatenate([jnp.exp(s_e - m_e), jnp.exp(s_o - m_o)], axis=1).astype(BF16)
                pv = _dot(p_cat, vals)
                den = pv[:, 2 * hd:4 * hd] + jnp.where(low_q, jnp.exp(sink_e - m_e), jnp.exp(sink_o - m_o))
                blocks.append(pv[:, 0:2 * hd] / den)
        attn_ref[n * L:(n + 1) * L, :] = jnp.concatenate(blocks, axis=1).astype(BF16)

    kv_ref[0:L, :] = kv_ref[tm:tm + L, :]
    mix = _dot(attn_ref[...], wo_ref[...]) + bo_ref[...]
    o_ref[...] = _layer_norm(alpha * x + mix, gain_ref[...], bias_ref[...])


def _swa_layer(h, batch, wqkv, bqkv, sinks, wo, bo, gain, bias, slot, alpha):
    n, d = h.shape
    seq = n // batch
    tm = min(TOKEN_TILE, seq)
    spb = seq // tm
    qkv_w = wqkv.shape[2]
    dq = wo.shape[1]
    kvw = ATTN_KV_HEADS * ATTN_HEAD_DIM
    return pl.pallas_call(
        functools.partial(_swa_kernel, alpha=alpha),
        grid=(batch, spb),
        in_specs=[
            pl.BlockSpec((tm, d), lambda b, s: (b * spb + s, 0)),
            _resident((None, d, qkv_w), lambda b, s: (slot, 0, 0)),
            _resident((1, qkv_w), lambda b, s: (0, 0)),
            pl.BlockSpec(memory_space=pltpu.SMEM),
            _resident((None, dq, d), lambda b, s: (slot, 0, 0)),
            _resident((1, d), lambda b, s: (0, 0)),
            _resident((1, d), lambda b, s: (0, 0)),
            _resident((1, d), lambda b, s: (0, 0)),
        ],
        out_specs=pl.BlockSpec((tm, d), lambda b, s: (b * spb + s, 0)),
        out_shape=jax.ShapeDtypeStruct((n, d), F32),
        scratch_shapes=[
            pltpu.VMEM((tm, dq), BF16),
            pltpu.VMEM((ATTN_BLOCK + tm, 2 * kvw), F32),
            pltpu.VMEM((tm, dq), BF16),
        ],
        compiler_params=_compiler_params(2),
        name=f"swa_s{slot}",
    )(h, wqkv, bqkv, sinks, wo, bo, gain, bias)


def kernel(x, ffn_w_gate_up, ffn_w_down, ln_gain, ln_bias, gmlp_w_in, gmlp_ln_gain, gmlp_ln_bias, gmlp_w_spatial, gmlp_b_spatial, gmlp_w_out, hgrn_w_in, hgrn_lb_logits, hgrn_norm_gain, hgrn_w_out, attn_w_qkv, attn_b_qkv, attn_sinks, attn_w_o, attn_b_o):
    batch, seq, d = x.shape
    depth = ffn_w_gate_up.shape[0]
    n_mixers = 3
    alpha = (2.0 * depth) ** 0.25

    wgu = ffn_w_gate_up.astype(BF16)
    wd = ffn_w_down.astype(BF16)
    gmlp_win = gmlp_w_in.astype(BF16)
    gmlp_wout = gmlp_w_out.astype(BF16)
    hgrn_win = hgrn_w_in.astype(BF16)
    hgrn_wout = hgrn_w_out.astype(BF16)
    attn_wqkv = attn_w_qkv.astype(BF16)
    attn_wo = attn_w_o.astype(BF16)

    h = x.reshape(batch * seq, d)
    for layer in range(depth):
        kind = layer % n_mixers
        slot = layer // n_mixers
        h = _ffn_layer(h, wgu, wd, ln_gain[layer, 0][None], ln_bias[layer, 0][None], layer, 0, alpha)
        gain, bias = ln_gain[layer, 1][None], ln_bias[layer, 1][None]
        if kind == 0:
            group_dim = gmlp_w_out.shape[1] // GMLP_GROUPS
            bs_full = jnp.repeat(jnp.transpose(gmlp_b_spatial[slot]), group_dim, axis=1)
            h = _gmlp_layer(h, gmlp_win, gmlp_ln_gain[slot][None], gmlp_ln_bias[slot][None],
                            gmlp_w_spatial, bs_full, gmlp_wout, gain, bias, slot, alpha)
        elif kind == 1:
            h = _hgrn_layer(h, batch, hgrn_win, hgrn_lb_logits, hgrn_norm_gain[slot][None],
                            hgrn_wout, gain, bias, slot, layer, alpha)
        else:
            h = _swa_layer(h, batch, attn_wqkv, attn_b_qkv[slot][None], attn_sinks[slot],
                           attn_wo, attn_b_o[slot][None], gain, bias, slot, alpha)
        h = _ffn_layer(h, wgu, wd, ln_gain[layer, 2][None], ln_bias[layer, 2][None], layer, 1, alpha)
    return h.reshape(batch, seq, d)
```

```python
import functools
import math

import jax
import jax.numpy as jnp
from jax import lax
from jax.experimental import pallas as pl
from jax.experimental.pallas import tpu as pltpu

F32 = jnp.float32
BF16 = jnp.bfloat16

LN_EPS = 1e-5
RMS_EPS = 1e-6

LANES = 128
VMEM_LIMIT_BYTES = 56 * 1024 * 1024

FFN_TOKEN_TILE = 1024
FFN_ROW_GROUP = 256
FFN_COL_TILE = 256
GMLP_TOKEN_TILE = 1024
GMLP_ROW_GROUP = 512
HGRN_TOKEN_TILE = 512
HGRN_ROW_GROUP = 256
SWA_TOKEN_TILE = 512
SWA_ROW_GROUP = 512

GMLP_CHUNK = 128
GMLP_GROUPS = 8
HGRN_HEADS = 8
HGRN_CHUNK = 128
ATTN_BLOCK = 128
ATTN_HEAD_DIM = 64
ATTN_Q_HEADS = 16
ATTN_KV_HEADS = 2


def _dot(a, b):
    return jnp.dot(a, b, preferred_element_type=F32)


def _dot_nt(a, b):
    return lax.dot_general(a, b, (((1,), (1,)), ((), ())), preferred_element_type=F32)


def _dot_tn(a, b):
    return lax.dot_general(a, b, (((0,), (0,)), ((), ())), preferred_element_type=F32)


def _layer_norm(y, gain, bias):
    mu = jnp.mean(y, axis=-1, keepdims=True)
    yc = y - mu
    var = jnp.mean(yc * yc, axis=-1, keepdims=True)
    return yc * lax.rsqrt(var + LN_EPS) * gain + bias


def _silu(x):
    return x * jax.nn.sigmoid(x)


def _gelu_erf(x):
    return 0.5 * x * (1.0 + lax.erf(x * math.sqrt(0.5)))


def _resident(block_shape, index_map):
    return pl.BlockSpec(block_shape, index_map, pipeline_mode=pl.Buffered(1))


def _compiler_params(n_grid_axes):
    return pltpu.CompilerParams(
        dimension_semantics=("arbitrary",) * n_grid_axes,
        vmem_limit_bytes=VMEM_LIMIT_BYTES,
    )


def _ffn_kernel(x_ref, wgu_ref, wd_ref, gain_ref, bias_ref, o_ref, act_ref, *, alpha):
    xb = x_ref[...].astype(BF16)
    d_ff = wd_ref.shape[0]
    for c in range(d_ff // FFN_COL_TILE):
        lo = c * FFN_COL_TILE
        gate = _dot(xb, wgu_ref[:, lo:lo + FFN_COL_TILE])
        up = _dot(xb, wgu_ref[:, d_ff + lo:d_ff + lo + FFN_COL_TILE])
        act_ref[:, lo:lo + FFN_COL_TILE] = (_silu(gate) * up).astype(BF16)
    tm = x_ref.shape[0]
    rg = min(FFN_ROW_GROUP, tm)
    for r in range(tm // rg):
        rows = slice(r * rg, (r + 1) * rg)
        ffn = _dot(act_ref[rows, :], wd_ref[...])
        o_ref[rows, :] = _layer_norm(alpha * x_ref[rows, :] + 0.5 * ffn, gain_ref[...], bias_ref[...])


def _ffn_layer(h, wgu, wd, gain, bias, layer, half, alpha):
    n, d = h.shape
    d_ff = wd.shape[2]
    tm = min(FFN_TOKEN_TILE, n)
    return pl.pallas_call(
        functools.partial(_ffn_kernel, alpha=alpha),
        grid=(n // tm,),
        in_specs=[
            pl.BlockSpec((tm, d), lambda i: (i, 0)),
            _resident((None, None, d, 2 * d_ff), lambda i: (layer, half, 0, 0)),
            _resident((None, None, d_ff, d), lambda i: (layer, half, 0, 0)),
            _resident((1, d), lambda i: (0, 0)),
            _resident((1, d), lambda i: (0, 0)),
        ],
        out_specs=pl.BlockSpec((tm, d), lambda i: (i, 0)),
        out_shape=jax.ShapeDtypeStruct((n, d), F32),
        scratch_shapes=[pltpu.VMEM((tm, d_ff), BF16)],
        compiler_params=_compiler_params(1),
        name=f"ffn_l{layer}_h{half}",
    )(h, wgu, wd, gain, bias)


def _gmlp_kernel(x_ref, win_ref, vg_ref, vb_ref, ws_ref, bs_ref, wout_ref, gain_ref, bias_ref,
                 o_ref, gated_ref, *, alpha):
    tm = x_ref.shape[0]
    width = wout_ref.shape[0]
    gd = width // GMLP_GROUPS
    rg = min(GMLP_ROW_GROUP, tm)
    n_chunks = rg // GMLP_CHUNK

    t_idx = lax.broadcasted_iota(jnp.int32, (GMLP_CHUNK, GMLP_CHUNK), 0)
    s_idx = lax.broadcasted_iota(jnp.int32, (GMLP_CHUNK, GMLP_CHUNK), 1)
    causal = s_idx <= t_idx
    w_s = [jnp.where(causal, ws_ref[g], 0.0).astype(BF16) for g in range(GMLP_GROUPS)]

    for r in range(tm // rg):
        rows = slice(r * rg, (r + 1) * rg)
        x = x_ref[rows, :]
        xb = x.astype(BF16)
        v = _gelu_erf(_dot(xb, win_ref[:, width:]))
        v = _layer_norm(v, vg_ref[...], vb_ref[...]).astype(BF16)
        for g in range(GMLP_GROUPS):
            v_g = jnp.concatenate(
                [v[c * GMLP_CHUNK:(c + 1) * GMLP_CHUNK, g * gd:(g + 1) * gd] for c in range(n_chunks)],
                axis=1)
            mixed = _dot(w_s[g], v_g)
            b_g = bs_ref[:, g * gd:(g + 1) * gd]
            for c in range(n_chunks):
                gated_ref[r * rg + c * GMLP_CHUNK:r * rg + (c + 1) * GMLP_CHUNK, g * gd:(g + 1) * gd] = (
                    mixed[:, c * gd:(c + 1) * gd] + b_g)
        u = _gelu_erf(_dot(xb, win_ref[:, :width]))
        mix = _dot((u * gated_ref[rows, :]).astype(BF16), wout_ref[...])
        o_ref[rows, :] = _layer_norm(alpha * x + mix, gain_ref[...], bias_ref[...])


def _gmlp_layer(h, win, vg, vb, ws, bs_full, wout, gain, bias, slot, alpha):
    n, d = h.shape
    width = wout.shape[1]
    tm = min(GMLP_TOKEN_TILE, n)
    return pl.pallas_call(
        functools.partial(_gmlp_kernel, alpha=alpha),
        grid=(n // tm,),
        in_specs=[
            pl.BlockSpec((tm, d), lambda i: (i, 0)),
            _resident((None, d, 2 * width), lambda i: (slot, 0, 0)),
            _resident((1, width), lambda i: (0, 0)),
            _resident((1, width), lambda i: (0, 0)),
            _resident((None, GMLP_GROUPS, GMLP_CHUNK, GMLP_CHUNK), lambda i: (slot, 0, 0, 0)),
            _resident((GMLP_CHUNK, width), lambda i: (0, 0)),
            _resident((None, width, d), lambda i: (slot, 0, 0)),
            _resident((1, d), lambda i: (0, 0)),
            _resident((1, d), lambda i: (0, 0)),
        ],
        out_specs=pl.BlockSpec((tm, d), lambda i: (i, 0)),
        out_shape=jax.ShapeDtypeStruct((n, d), F32),
        scratch_shapes=[pltpu.VMEM((tm, width), F32)],
        compiler_params=_compiler_params(1),
        name=f"gmlp_s{slot}",
    )(h, win, vg, vb, ws, bs_full, wout, gain, bias)


def _split3_bf16(x):
    hi = x.astype(BF16)
    r1 = x - hi.astype(F32)
    mid = r1.astype(BF16)
    lo = (r1 - mid.astype(F32)).astype(BF16)
    return hi, mid, lo


def _hgrn_kernel(x_ref, win_ref, lbl_ref, ng_ref, wout_ref, gain_ref, bias_ref, o_ref,
                 state_ref, *, alpha, layer):
    @pl.when(pl.program_id(1) == 0)
    def _():
        state_ref[...] = jnp.zeros_like(state_ref)

    tm = x_ref.shape[0]
    width = wout_ref.shape[0]
    dk = width // HGRN_HEADS
    rg = min(HGRN_ROW_GROUP, tm)

    logits = lbl_ref[...]
    e = jnp.exp(logits - jnp.max(logits, axis=0, keepdims=True))
    probs = e / jnp.sum(e, axis=0, keepdims=True)
    lb = jnp.zeros((1, width), F32)
    for j in range(1, layer + 1):
        lb = lb + probs[j:j + 1, :]

    row = lax.broadcasted_iota(jnp.int32, (HGRN_CHUNK, HGRN_CHUNK), 0)
    col = lax.broadcasted_iota(jnp.int32, (HGRN_CHUNK, HGRN_CHUNK), 1)
    lower = col <= row
    tri = jnp.where(lower, 1.0, 0.0).astype(BF16)
    tri3 = jnp.concatenate([tri, tri, tri], axis=1)
    norm_g = ng_ref[...]

    def chunk_step(q, f, v, gate):
        gc = _dot(tri3, jnp.concatenate(_split3_bf16(jnp.log(f)), axis=0))
        g_last = gc[HGRN_CHUNK - 1:HGRN_CHUNK, :]
        g_mid = 0.5 * g_last
        e_mid = jnp.exp(g_mid)
        q_t = q * jnp.exp(gc - g_mid)
        k_t = (1.0 - f) * jnp.exp(g_mid - gc)
        q_d = (q_t * e_mid).astype(BF16)
        k_d = (k_t * e_mid).astype(BF16)
        q_t = q_t.astype(BF16)
        k_t = k_t.astype(BF16)
        v = v.astype(BF16)
        decay = jnp.exp(g_last)
        outs = []
        for h in range(HGRN_HEADS):
            hs = slice(h * dk, (h + 1) * dk)
            scores = jnp.where(lower, _dot_nt(q_t[:, hs], k_t[:, hs]), 0.0)
            st = state_ref[h]
            o_h = _dot(scores.astype(BF16), v[:, hs]) + _dot_nt(q_d[:, hs], st.astype(BF16))
            state_ref[h] = st * decay[:, hs] + _dot_tn(v[:, hs], k_d[:, hs])
            ms = jnp.mean(o_h * o_h, axis=-1, keepdims=True)
            outs.append(o_h * lax.rsqrt(ms + RMS_EPS) * norm_g)
        return (jnp.concatenate(outs, axis=1) * gate).astype(BF16)

    for r in range(tm // rg):
        rows = slice(r * rg, (r + 1) * rg)
        x = x_ref[rows, :]
        xb = x.astype(BF16)
        q = _silu(_dot(xb, win_ref[:, 0:width]))
        f = lb + (1.0 - lb) * jax.nn.sigmoid(_dot(xb, win_ref[:, width:2 * width]))
        v = _dot(xb, win_ref[:, 2 * width:3 * width])
        gate = _silu(_dot(xb, win_ref[:, 3 * width:4 * width]))
        o = jnp.concatenate(
            [chunk_step(q[cs], f[cs], v[cs], gate[cs])
             for cs in (slice(c * HGRN_CHUNK, (c + 1) * HGRN_CHUNK) for c in range(rg // HGRN_CHUNK))],
            axis=0)
        mix = _dot(o, wout_ref[...])
        o_ref[rows, :] = _layer_norm(alpha * x + mix, gain_ref[...], bias_ref[...])


def _hgrn_layer(h, batch, win, lb_logits, norm_g, wout, gain, bias, slot, layer, alpha):
    n, d = h.shape
    width = wout.shape[1]
    seq = n // batch
    tm = min(HGRN_TOKEN_TILE, seq)
    spb = seq // tm
    depth = lb_logits.shape[0]
    return pl.pallas_call(
        functools.partial(_hgrn_kernel, alpha=alpha, layer=layer),
        grid=(batch, spb),
        in_specs=[
            pl.BlockSpec((tm, d), lambda b, s: (b * spb + s, 0)),
            _resident((None, d, 4 * width), lambda b, s: (slot, 0, 0)),
            _resident((depth, width), lambda b, s: (0, 0)),
            _resident((1, width // HGRN_HEADS), lambda b, s: (0, 0)),
            _resident((None, width, d), lambda b, s: (slot, 0, 0)),
            _resident((1, d), lambda b, s: (0, 0)),
            _resident((1, d), lambda b, s: (0, 0)),
        ],
        out_specs=pl.BlockSpec((tm, d), lambda b, s: (b * spb + s, 0)),
        out_shape=jax.ShapeDtypeStruct((n, d), F32),
        scratch_shapes=[
            pltpu.VMEM((HGRN_HEADS, width // HGRN_HEADS, width // HGRN_HEADS), F32),
        ],
        compiler_params=_compiler_params(2),
        name=f"hgrn_s{slot}",
    )(h, win, lb_logits, norm_g, wout, gain, bias)


def _swa_kernel(x_ref, wqkv_ref, bqkv_ref, sink_ref, wo_ref, bo_ref, gain_ref, bias_ref, o_ref,
                kv_ref, *, alpha):
    L = ATTN_BLOCK
    hd = ATTN_HEAD_DIM
    group = ATTN_Q_HEADS // ATTN_KV_HEADS
    pairs = group // 2
    kvw = ATTN_KV_HEADS * hd
    dq = ATTN_Q_HEADS * hd
    first_tile = pl.program_id(1) == 0

    @pl.when(first_tile)
    def _():
        kv_ref[0:L, :] = jnp.zeros((L, 2 * kvw), F32)

    tm = x_ref.shape[0]
    rg = min(SWA_ROW_GROUP, tm)

    lane =lax.broadcasted_iota(jnp.int32, (2 * L, kvw), 1)
    low = lane < hd
    qi = lax.broadcasted_iota(jnp.int32, (L, 2 * L), 0)
    kj = lax.broadcasted_iota(jnp.int32, (L, 2 * L), 1)
    in_band = (kj > qi) & (kj <= qi + L)
    band_bias = jnp.where(in_band, 0.0, -jnp.inf)
    first_bias = jnp.where(in_band & (kj >= L), 0.0, -jnp.inf)
    ones_cols = jnp.where(low, 1.0, 0.0)
    low_q = lax.broadcasted_iota(jnp.int32, (L, 2 * hd), 1) < hd

    def attend(qb, kv2, bias):
        k2 = kv2[:, 0:kvw]
        v2 = kv2[:, kvw:2 * kvw]
        k2r = pltpu.roll(k2, hd, axis=1)
        v2r = pltpu.roll(v2, hd, axis=1)
        blocks = []
        for hk in range(ATTN_KV_HEADS):
            k_lo, k_hi = (k2, k2r) if hk == 0 else (k2r, k2)
            v_lo, v_hi = (v2, v2r) if hk == 0 else (v2r, v2)
            keys = jnp.concatenate([jnp.where(low, k_lo, 0.0), jnp.where(low, 0.0, k_hi)],
                                   axis=0).astype(BF16)
            vals = jnp.concatenate(
                [jnp.concatenate([jnp.where(low, v_lo, 0.0), ones_cols], axis=1),
                 jnp.concatenate([jnp.where(low, 0.0, v_hi), 1.0 - ones_cols], axis=1)],
                axis=0).astype(BF16)
            q_pairs = jnp.concatenate(
                [qb[:, (hk * pairs + p) * 2 * hd:(hk * pairs + p + 1) * 2 * hd] for p in range(pairs)],
                axis=0)
            s_all = _dot_nt(q_pairs, keys)
            for p in range(pairs):
                head = hk * group + 2 * p
                sink_e = sink_ref[head]
                sink_o = sink_ref[head + 1]
                s_e = s_all[p * L:(p + 1) * L, 0:2 * L] + bias
                s_o = s_all[p * L:(p + 1) * L, 2 * L:4 * L] + bias
                m_e = jnp.maximum(jnp.max(s_e, axis=-1, keepdims=True), sink_e)
                m_o = jnp.maximum(jnp.max(s_o, axis=-1, keepdims=True), sink_o)
                p_cat = jnp.concatenate([jnp.exp(s_e - m_e), jnp.exp(s_o - m_o)], axis=1).astype(BF16)
                pv = _dot(p_cat, vals)
                den = pv[:, 2 * hd:4 * hd] + jnp.where(low_q, jnp.exp(sink_e - m_e), jnp.exp(sink_o - m_o))
                blocks.append(pv[:, 0:2 * hd] / den)
        return jnp.concatenate(blocks, axis=1).astype(BF16)

    for r in range(tm // rg):
        rows = slice(r * rg, (r + 1) * rg)
        x = x_ref[rows, :]
        xb = x.astype(BF16)
        q = ((_dot(xb, wqkv_ref[:, 0:dq]) + bqkv_ref[:, 0:dq]) * (hd ** -0.5)).astype(BF16)
        kv_ref[L + r * rg:L + (r + 1) * rg, :] = (
            _dot(xb, wqkv_ref[:, dq:dq + 2 * kvw]) + bqkv_ref[:, dq:dq + 2 * kvw])
        outs = []
        for nb in range(rg // L):
            n = r * (rg // L) + nb
            bias = jnp.where(first_tile, first_bias, band_bias) if n == 0 else band_bias
            outs.append(attend(q[nb * L:(nb + 1) * L, :], kv_ref[n * L:(n + 2) * L, :], bias))
        mix = _dot(jnp.concatenate(outs, axis=0), wo_ref[...]) + bo_ref[...]
        o_ref[rows, :] = _layer_norm(alpha * x + mix, gain_ref[...], bias_ref[...])

    kv_ref[0:L, :] = kv_ref[tm:tm + L, :]


def _swa_layer(h, batch, wqkv, bqkv, sinks, wo, bo, gain, bias, slot, alpha):
    n, d = h.shape
    seq = n // batch
    tm = min(SWA_TOKEN_TILE, seq)
    spb = seq // tm
    qkv_w = wqkv.shape[2]
    dq = wo.shape[1]
    kvw = ATTN_KV_HEADS * ATTN_HEAD_DIM
    return pl.pallas_call(
        functools.partial(_swa_kernel, alpha=alpha),
        grid=(batch, spb),
        in_specs=[
            pl.BlockSpec((tm, d), lambda b, s: (b * spb + s, 0)),
            _resident((None, d, qkv_w), lambda b, s: (slot, 0, 0)),
            _resident((1, qkv_w), lambda b, s: (0, 0)),
            pl.BlockSpec(memory_space=pltpu.SMEM),
            _resident((None, dq, d), lambda b, s: (slot, 0, 0)),
            _resident((1, d), lambda b, s: (0, 0)),
            _resident((1, d), lambda b, s: (0, 0)),
            _resident((1, d), lambda b, s: (0, 0)),
        ],
        out_specs=pl.BlockSpec((tm, d), lambda b, s: (b * spb + s, 0)),
        out_shape=jax.ShapeDtypeStruct((n, d), F32),
        scratch_shapes=[
            pltpu.VMEM((ATTN_BLOCK + tm, 2 * kvw), F32),
        ],
        compiler_params=_compiler_params(2),
        name=f"swa_s{slot}",
    )(h, wqkv, bqkv, sinks, wo, bo, gain, bias)


def kernel(x, ffn_w_gate_up, ffn_w_down, ln_gain, ln_bias, gmlp_w_in, gmlp_ln_gain, gmlp_ln_bias, gmlp_w_spatial, gmlp_b_spatial, gmlp_w_out, hgrn_w_in, hgrn_lb_logits, hgrn_norm_gain, hgrn_w_out, attn_w_qkv, attn_b_qkv, attn_sinks, attn_w_o, attn_b_o):
    batch, seq, d = x.shape
    depth = ffn_w_gate_up.shape[0]
    n_mixers = 3
    alpha = (2.0 * depth) ** 0.25

    wgu = ffn_w_gate_up.astype(BF16)
    wd = ffn_w_down.astype(BF16)
    gmlp_win = gmlp_w_in.astype(BF16)
    gmlp_wout = gmlp_w_out.astype(BF16)
    hgrn_win = hgrn_w_in.astype(BF16)
    hgrn_wout = hgrn_w_out.astype(BF16)
    attn_wqkv = attn_w_qkv.astype(BF16)
    attn_wo = attn_w_o.astype(BF16)

    h = x.reshape(batch * seq, d)
    for layer in range(depth):
        kind = layer % n_mixers
        slot = layer // n_mixers
        h = _ffn_layer(h, wgu, wd, ln_gain[layer, 0][None], ln_bias[layer, 0][None], layer, 0, alpha)
        gain, bias = ln_gain[layer, 1][None], ln_bias[layer, 1][None]
        if kind == 0:
            group_dim = gmlp_w_out.shape[1] // GMLP_GROUPS
            bs_full = jnp.repeat(jnp.transpose(gmlp_b_spatial[slot]), group_dim, axis=1)
            h = _gmlp_layer(h, gmlp_win, gmlp_ln_gain[slot][None], gmlp_ln_bias[slot][None],
                            gmlp_w_spatial, bs_full, gmlp_wout, gain, bias, slot, alpha)
        elif kind == 1:
            h = _hgrn_layer(h, batch, hgrn_win, hgrn_lb_logits, hgrn_norm_gain[slot][None],
                            hgrn_wout, gain, bias, slot, layer, alpha)
        else:
            h = _swa_layer(h, batch, attn_wqkv, attn_b_qkv[slot][None], attn_sinks[slot],
                           attn_wo, attn_b_o[slot][None], gain, bias, slot, alpha)
        h = _ffn_layer(h, wgu, wd, ln_gain[layer, 2][None], ln_bias[layer, 2][None], layer, 1, alpha)
    return h.reshape(batch, seq, d)
```

```python
import functools
import math

import jax
import jax.numpy as jnp
from jax import lax
from jax.experimental import pallas as pl
from jax.experimental.pallas import tpu as pltpu

F32 = jnp.float32
BF16 = jnp.bfloat16

LN_EPS = 1e-5
RMS_EPS = 1e-6

LANES = 128
SUBLANES = 8
VMEM_LIMIT_BYTES = 56 * 1024 * 1024

FFN_TOKEN_TILE = 1024
OUT_ROW_GROUP = 256
FFN_COL_TILE = 256
FFN_WEIGHT_CHUNKS = 8
GMLP_TOKEN_TILE = 1024
GMLP_ROW_GROUP = 512
HGRN_TOKEN_TILE = 512
HGRN_ROW_GROUP = 256
SWA_TOKEN_TILE = 512
SWA_ROW_GROUP = 512

GMLP_CHUNK = 128
GMLP_GROUPS = 8
HGRN_HEADS = 8
HGRN_CHUNK = 128
ATTN_BLOCK = 128
ATTN_HEAD_DIM = 64
ATTN_Q_HEADS = 16
ATTN_KV_HEADS = 2


def _dot(a, b):
    return jnp.dot(a, b, preferred_element_type=F32)


def _dot_nt(a, b):
    return lax.dot_general(a, b, (((1,), (1,)), ((), ())), preferred_element_type=F32)


def _dot_tn(a, b):
    return lax.dot_general(a, b, (((0,), (0,)), ((), ())), preferred_element_type=F32)


def _layer_norm(y, gain, bias):
    mu = jnp.mean(y, axis=-1, keepdims=True)
    yc = y - mu
    var = jnp.mean(yc * yc, axis=-1, keepdims=True)
    return yc * lax.rsqrt(var + LN_EPS) * gain + bias


def _silu(x):
    return x * jax.nn.sigmoid(x)


def _gelu_erf(x):
    return 0.5 * x * (1.0 + lax.erf(x * math.sqrt(0.5)))


def _project_add_norm(o_ref, x_ref, row0, act, w_ref, b_out, gain_ref, bias_ref, alpha, scale):
    n = act.shape[0]
    rg = min(OUT_ROW_GROUP, n)
    for r in range(n // rg):
        mix = _dot(act[r * rg:(r + 1) * rg, :], w_ref[...])
        if b_out is not None:
            mix = mix + b_out
        if scale != 1.0:
            mix = scale * mix
        rows = slice(row0 + r * rg, row0 + (r + 1) * rg)
        o_ref[rows, :] = _layer_norm(alpha * x_ref[rows, :] + mix, gain_ref[...], bias_ref[...])


def _resident(block_shape, index_map):
    return pl.BlockSpec(block_shape, index_map, pipeline_mode=pl.Buffered(1))


def _compiler_params(n_grid_axes):
    return pltpu.CompilerParams(
        dimension_semantics=("arbitrary",) * n_grid_axes,
        vmem_limit_bytes=VMEM_LIMIT_BYTES,
    )


def _weight_chunk_copy(w_hbm, stage_ref, sem_ref, c):
    rows = stage_ref.shape[1]
    return pltpu.make_async_copy(w_hbm.at[pl.ds(c * rows, rows), :], stage_ref.at[c % 2], sem_ref.at[c % 2])


def _load_weights_as_bf16(w_hbm, w_ref, stage_ref, sem_ref):
    rows = stage_ref.shape[1]
    n_chunks = w_ref.shape[0] // rows
    _weight_chunk_copy(w_hbm, stage_ref, sem_ref, 0).start()
    for c in range(n_chunks):
        if c + 1 < n_chunks:
            _weight_chunk_copy(w_hbm, stage_ref, sem_ref, c + 1).start()
        _weight_chunk_copy(w_hbm, stage_ref, sem_ref, c).wait()
        w_ref[c * rows:(c + 1) * rows, :] = stage_ref[c % 2].astype(BF16)


def _ffn_kernel(x_ref, wgu_hbm, wd_hbm, gain_ref, bias_ref, o_ref,
                wgu_ref, wd_ref, act_ref, stage_gu_ref, stage_d_ref, sem_gu_ref, sem_d_ref,
                *, alpha, layer, half):
    @pl.when(pl.program_id(0) == 0)
    def _():
        _load_weights_as_bf16(wgu_hbm.at[layer, half], wgu_ref, stage_gu_ref, sem_gu_ref)
        _load_weights_as_bf16(wd_hbm.at[layer, half], wd_ref, stage_d_ref, sem_d_ref)

    xb = x_ref[...].astype(BF16)
    d_ff = wd_ref.shape[0]
    for c in range(d_ff // FFN_COL_TILE):
        lo = c * FFN_COL_TILE
        gate = _dot(xb, wgu_ref[:, lo:lo + FFN_COL_TILE])
        up = _dot(xb, wgu_ref[:, d_ff + lo:d_ff + lo + FFN_COL_TILE])
        act_ref[:, lo:lo + FFN_COL_TILE] = (_silu(gate) * up).astype(BF16)
    _project_add_norm(o_ref, x_ref, 0, act_ref, wd_ref, None, gain_ref, bias_ref, alpha, 0.5)


def _ffn_layer(h, wgu, wd, gain, bias, layer, half, alpha):
    n, d = h.shape
    d_ff = wd.shape[2]
    tm = min(FFN_TOKEN_TILE, n)
    return pl.pallas_call(
        functools.partial(_ffn_kernel, alpha=alpha, layer=layer, half=half),
        grid=(n // tm,),
        in_specs=[
            pl.BlockSpec((tm, d), lambda i: (i, 0)),
            pl.BlockSpec(memory_space=pl.ANY),
            pl.BlockSpec(memory_space=pl.ANY),
            _resident((1, d), lambda i: (0, 0)),
            _resident((1, d), lambda i: (0, 0)),
        ],
        out_specs=pl.BlockSpec((tm, d), lambda i: (i, 0)),
        out_shape=jax.ShapeDtypeStruct((n, d), F32),
        scratch_shapes=[
            pltpu.VMEM((d, 2 * d_ff), BF16),
            pltpu.VMEM((d_ff, d), BF16),
            pltpu.VMEM((tm, d_ff), BF16),
            pltpu.VMEM((2, d // FFN_WEIGHT_CHUNKS, 2 * d_ff), F32),
            pltpu.VMEM((2, d_ff // FFN_WEIGHT_CHUNKS, d), F32),
            pltpu.SemaphoreType.DMA((2,)),
            pltpu.SemaphoreType.DMA((2,)),
        ],
        compiler_params=_compiler_params(1),
        name=f"ffn_l{layer}_h{half}",
    )(h, wgu, wd, gain, bias)


def _gmlp_kernel(x_ref, win_ref, vg_ref, vb_ref, ws_ref, bs_ref, wout_ref, gain_ref, bias_ref,
                 o_ref, gated_ref, *, alpha):
    tm = x_ref.shape[0]
    width = wout_ref.shape[0]
    gd = width // GMLP_GROUPS
    rg = min(GMLP_ROW_GROUP, tm)
    n_chunks = rg // GMLP_CHUNK

    t_idx = lax.broadcasted_iota(jnp.int32, (GMLP_CHUNK, GMLP_CHUNK), 0)
    s_idx = lax.broadcasted_iota(jnp.int32, (GMLP_CHUNK, GMLP_CHUNK), 1)
    causal = s_idx <= t_idx
    w_s = [jnp.where(causal, ws_ref[g], 0.0).astype(BF16) for g in range(GMLP_GROUPS)]

    for r in range(tm // rg):
        rows = slice(r * rg, (r + 1) * rg)
        xb = x_ref[rows, :].astype(BF16)
        v = _gelu_erf(_dot(xb, win_ref[:, width:]))
        u = _gelu_erf(_dot(xb, win_ref[:, :width]))
        v = _layer_norm(v, vg_ref[...], vb_ref[...]).astype(BF16)
        for g in range(GMLP_GROUPS):
            v_g = jnp.concatenate(
                [v[c * GMLP_CHUNK:(c + 1) * GMLP_CHUNK, g * gd:(g + 1) * gd] for c in range(n_chunks)],
                axis=1)
            mixed = _dot(w_s[g], v_g)
            b_g = bs_ref[:, g * gd:(g + 1) * gd]
            for c in range(n_chunks):
                gated_ref[r * rg + c * GMLP_CHUNK:r * rg + (c + 1) * GMLP_CHUNK, g * gd:(g + 1) * gd] = (
                    mixed[:, c * gd:(c + 1) * gd] + b_g)
        _project_add_norm(o_ref, x_ref, r * rg, (u * gated_ref[rows, :]).astype(BF16), wout_ref, None,
                          gain_ref, bias_ref, alpha, 1.0)


def _gmlp_layer(h, win, vg, vb, ws, bs_full, wout, gain, bias, slot, alpha):
    n, d = h.shape
    width = wout.shape[1]
    tm = min(GMLP_TOKEN_TILE, n)
    return pl.pallas_call(
        functools.partial(_gmlp_kernel, alpha=alpha),
        grid=(n // tm,),
        in_specs=[
            pl.BlockSpec((tm, d), lambda i: (i, 0)),
            _resident((None, d, 2 * width), lambda i: (slot, 0, 0)),
            _resident((1, width), lambda i: (0, 0)),
            _resident((1, width), lambda i: (0, 0)),
            _resident((None, GMLP_GROUPS, GMLP_CHUNK, GMLP_CHUNK), lambda i: (slot, 0, 0, 0)),
            _resident((GMLP_CHUNK, width), lambda i: (0, 0)),
            _resident((None, width, d), lambda i: (slot, 0, 0)),
            _resident((1, d), lambda i: (0, 0)),
            _resident((1, d), lambda i: (0, 0)),
        ],
        out_specs=pl.BlockSpec((tm, d), lambda i: (i, 0)),
        out_shape=jax.ShapeDtypeStruct((n, d), F32),
        scratch_shapes=[pltpu.VMEM((tm, width), F32)],
        compiler_params=_compiler_params(1),
        name=f"gmlp_s{slot}",
    )(h, win, vg, vb, ws, bs_full, wout, gain, bias)


def _cumsum_rows(x):
    n = x.shape[0]
    row = lax.broadcasted_iota(jnp.int32, x.shape, 0)
    shift = 1
    while shift < n:
        if shift % SUBLANES:
            x = x + jnp.where(row >= shift, pltpu.roll(x, shift, axis=0), 0.0)
        else:
            x = jnp.concatenate([x[:shift], x[shift:] + x[:n - shift]], axis=0)
        shift *= 2
    return x


def _hgrn_kernel(x_ref, win_ref, lbl_ref, ng_ref, wout_ref, gain_ref, bias_ref, o_ref,
                 state_ref, *, alpha, layer):
    @pl.when(pl.program_id(1) == 0)
    def _():
        state_ref[...] = jnp.zeros_like(state_ref)

    tm = x_ref.shape[0]
    width = wout_ref.shape[0]
    dk = width // HGRN_HEADS
    rg = min(HGRN_ROW_GROUP, tm)

    logits = lbl_ref[...]
    e = jnp.exp(logits - jnp.max(logits, axis=0, keepdims=True))
    probs = e / jnp.sum(e, axis=0, keepdims=True)
    lb = jnp.zeros((1, width), F32)
    for j in range(1, layer + 1):
        lb = lb + probs[j:j + 1, :]

    row = lax.broadcasted_iota(jnp.int32, (HGRN_CHUNK, HGRN_CHUNK), 0)
    col = lax.broadcasted_iota(jnp.int32, (HGRN_CHUNK, HGRN_CHUNK), 1)
    lower = col <= row
    norm_g = ng_ref[...]

    def chunk_step(q, f, v, gate):
        gc = _cumsum_rows(jnp.log(f))
        g_last = gc[HGRN_CHUNK - 1:HGRN_CHUNK, :]
        g_mid = 0.5 * g_last
        e_mid = jnp.exp(g_mid)
        q_t = q * jnp.exp(gc - g_mid)
        k_t = (1.0 - f) * jnp.exp(g_mid - gc)
        q_d = (q_t * e_mid).astype(BF16)
        k_d = (k_t * e_mid).astype(BF16)
        q_t = q_t.astype(BF16)
        k_t = k_t.astype(BF16)
        v = v.astype(BF16)
        decay = jnp.exp(g_last)
        outs = []
        for h in range(HGRN_HEADS):
            hs = slice(h * dk, (h + 1) * dk)
            scores = jnp.where(lower, _dot_nt(q_t[:, hs], k_t[:, hs]), 0.0)
            st = state_ref[h]
            o_h = _dot(scores.astype(BF16), v[:, hs]) + _dot_nt(q_d[:, hs], st.astype(BF16))
            state_ref[h] = st * decay[:, hs] + _dot_tn(v[:, hs], k_d[:, hs])
            ms = jnp.mean(o_h * o_h, axis=-1, keepdims=True)
            outs.append(o_h * lax.rsqrt(ms + RMS_EPS) * norm_g)
        return (jnp.concatenate(outs, axis=1) * gate).astype(BF16)

    for r in range(tm // rg):
        rows = slice(r * rg, (r + 1) * rg)
        xb = x_ref[rows, :].astype(BF16)
        q = _silu(_dot(xb, win_ref[:, 0:width]))
        f = lb + (1.0 - lb) * jax.nn.sigmoid(_dot(xb, win_ref[:, width:2 * width]))
        v = _dot(xb, win_ref[:, 2 * width:3 * width])
        gate = _silu(_dot(xb, win_ref[:, 3 * width:4 * width]))
        o = jnp.concatenate(
            [chunk_step(q[cs], f[cs], v[cs], gate[cs])
             for cs in (slice(c * HGRN_CHUNK, (c + 1) * HGRN_CHUNK) for c in range(rg // HGRN_CHUNK))],
            axis=0)
        _project_add_norm(o_ref, x_ref, r * rg, o, wout_ref, None, gain_ref, bias_ref, alpha, 1.0)


def _hgrn_layer(h, batch, win, lb_logits, norm_g, wout, gain, bias, slot, layer, alpha):
    n, d = h.shape
    width = wout.shape[1]
    seq = n // batch
    tm = min(HGRN_TOKEN_TILE, seq)
    spb = seq // tm
    depth = lb_logits.shape[0]
    return pl.pallas_call(
        functools.partial(_hgrn_kernel, alpha=alpha, layer=layer),
        grid=(batch, spb),
        in_specs=[
            pl.BlockSpec((tm, d), lambda b, s: (b * spb + s, 0)),
            _resident((None, d, 4 * width), lambda b, s: (slot, 0, 0)),
            _resident((depth, width), lambda b, s: (0, 0)),
            _resident((1, width // HGRN_HEADS), lambda b, s: (0, 0)),
            _resident((None, width, d), lambda b, s: (slot, 0, 0)),
            _resident((1, d), lambda b, s: (0, 0)),
            _resident((1, d), lambda b, s: (0, 0)),
        ],
        out_specs=pl.BlockSpec((tm, d), lambda b, s: (b * spb + s, 0)),
        out_shape=jax.ShapeDtypeStruct((n, d), F32),
        scratch_shapes=[
            pltpu.VMEM((HGRN_HEADS, width // HGRN_HEADS, width // HGRN_HEADS), F32),
        ],
        compiler_params=_compiler_params(2),
        name=f"hgrn_s{slot}",
    )(h, win, lb_logits, norm_g, wout, gain, bias)


def _swa_kernel(x_ref, wqkv_ref, bqkv_ref, sink_ref, wo_ref, bo_ref, gain_ref, bias_ref, o_ref,
                kv_ref, *, alpha):
    L = ATTN_BLOCK
    hd = ATTN_HEAD_DIM
    group = ATTN_Q_HEADS // ATTN_KV_HEADS
    pairs = group // 2
    kvw = ATTN_KV_HEADS * hd
    dq = ATTN_Q_HEADS * hd
    first_tile = pl.program_id(1) == 0

    @pl.when(first_tile)
    def _():
        kv_ref[0:L, :] = jnp.zeros((L, 2 * kvw), F32)

    tm = x_ref.shape[0]
    rg = min(SWA_ROW_GROUP, tm)

    lane =lax.broadcasted_iota(jnp.int32, (2 * L, kvw), 1)
    low = lane < hd
    qi = lax.broadcasted_iota(jnp.int32, (L, 2 * L), 0)
    kj = lax.broadcasted_iota(jnp.int32, (L, 2 * L), 1)
    in_band = (kj > qi) & (kj <= qi + L)
    band_bias = jnp.where(in_band, 0.0, -jnp.inf)
    first_bias = jnp.where(in_band & (kj >= L), 0.0, -jnp.inf)
    ones_cols = jnp.where(low, 1.0, 0.0)
    low_q = lax.broadcasted_iota(jnp.int32, (L, 2 * hd), 1) < hd

    def attend(qb, kv2, bias):
        k2 = kv2[:, 0:kvw]
        v2 = kv2[:, kvw:2 * kvw]
        k2r = pltpu.roll(k2, hd, axis=1)
        v2r = pltpu.roll(v2, hd, axis=1)
        blocks = []
        for hk in range(ATTN_KV_HEADS):
            k_lo, k_hi = (k2, k2r) if hk == 0 else (k2r, k2)
            v_lo, v_hi = (v2, v2r) if hk == 0 else (v2r, v2)
            keys = jnp.concatenate([jnp.where(low, k_lo, 0.0), jnp.where(low, 0.0, k_hi)],
                                   axis=0).astype(BF16)
            vals = jnp.concatenate(
                [jnp.concatenate([jnp.where(low, v_lo, 0.0), ones_cols], axis=1),
                 jnp.concatenate([jnp.where(low, 0.0, v_hi), 1.0 - ones_cols], axis=1)],
                axis=0).astype(BF16)
            q_pairs = jnp.concatenate(
                [qb[:, (hk * pairs + p) * 2 * hd:(hk * pairs + p + 1) * 2 * hd] for p in range(pairs)],
                axis=0)
            s_all = _dot_nt(q_pairs, keys)
            probs, sink_terms = [], []
            for p in range(pairs):
                head = hk * group + 2 * p
                sink_e = sink_ref[head]
                sink_o = sink_ref[head + 1]
                s_e = s_all[p * L:(p + 1) * L, 0:2 * L] + bias
                s_o = s_all[p * L:(p + 1) * L, 2 * L:4 * L] + bias
                m_e = jnp.maximum(jnp.max(s_e, axis=-1, keepdims=True), sink_e)
                m_o = jnp.maximum(jnp.max(s_o, axis=-1, keepdims=True), sink_o)
                probs.append(jnp.concatenate([jnp.exp(s_e - m_e), jnp.exp(s_o - m_o)], axis=1).astype(BF16))
                sink_terms.append(jnp.where(low_q, jnp.exp(sink_e - m_e), jnp.exp(sink_o - m_o)))
            pv = _dot(jnp.concatenate(probs, axis=0), vals)
            for p in range(pairs):
                pv_p = pv[p * L:(p + 1) * L, :]
                blocks.append(pv_p[:, 0:2 * hd] / (pv_p[:, 2 * hd:4 * hd] + sink_terms[p]))
        return jnp.concatenate(blocks, axis=1).astype(BF16)

    for r in range(tm // rg):
        rows = slice(r * rg, (r + 1) * rg)
        xb = x_ref[rows, :].astype(BF16)
        q = ((_dot(xb, wqkv_ref[:, 0:dq]) + bqkv_ref[:, 0:dq]) * (hd ** -0.5)).astype(BF16)
        kv_ref[L + r * rg:L + (r + 1) * rg, :] = (
            _dot(xb, wqkv_ref[:, dq:dq + 2 * kvw]) + bqkv_ref[:, dq:dq + 2 * kvw])
        outs = []
        for nb in range(rg // L):
            n = r * (rg // L) + nb
            bias = jnp.where(first_tile, first_bias, band_bias) if n == 0 else band_bias
            outs.append(attend(q[nb * L:(nb + 1) * L, :], kv_ref[n * L:(n + 2) * L, :], bias))
        _project_add_norm(o_ref, x_ref, r * rg, jnp.concatenate(outs, axis=0), wo_ref, bo_ref[...],
                          gain_ref, bias_ref, alpha, 1.0)

    kv_ref[0:L, :] = kv_ref[tm:tm + L, :]


def _swa_layer(h, batch, wqkv, bqkv, sinks, wo, bo, gain, bias, slot, alpha):
    n, d = h.shape
    seq = n // batch
    tm = min(SWA_TOKEN_TILE, seq)
    spb = seq // tm
    qkv_w = wqkv.shape[2]
    dq = wo.shape[1]
    kvw = ATTN_KV_HEADS * ATTN_HEAD_DIM
    return pl.pallas_call(
        functools.partial(_swa_kernel, alpha=alpha),
        grid=(batch, spb),
        in_specs=[
            pl.BlockSpec((tm, d), lambda b, s: (b * spb + s, 0)),
            _resident((None, d, qkv_w), lambda b, s: (slot, 0, 0)),
            _resident((1, qkv_w), lambda b, s: (0, 0)),
            pl.BlockSpec(memory_space=pltpu.SMEM),
            _resident((None, dq, d), lambda b, s: (slot, 0, 0)),
            _resident((1, d), lambda b, s: (0, 0)),
            _resident((1, d), lambda b, s: (0, 0)),
            _resident((1, d), lambda b, s: (0, 0)),
        ],
        out_specs=pl.BlockSpec((tm, d), lambda b, s: (b * spb + s, 0)),
        out_shape=jax.ShapeDtypeStruct((n, d), F32),
        scratch_shapes=[
            pltpu.VMEM((ATTN_BLOCK + tm, 2 * kvw), F32),
        ],
        compiler_params=_compiler_params(2),
        name=f"swa_s{slot}",
    )(h, wqkv, bqkv, sinks, wo, bo, gain, bias)


def kernel(x, ffn_w_gate_up, ffn_w_down, ln_gain, ln_bias, gmlp_w_in, gmlp_ln_gain, gmlp_ln_bias, gmlp_w_spatial, gmlp_b_spatial, gmlp_w_out, hgrn_w_in, hgrn_lb_logits, hgrn_norm_gain, hgrn_w_out, attn_w_qkv, attn_b_qkv, attn_sinks, attn_w_o, attn_b_o):
    batch, seq, d = x.shape
    depth = ffn_w_gate_up.shape[0]
    n_mixers = 3
    alpha = (2.0 * depth) ** 0.25

    wgu = ffn_w_gate_up
    wd = ffn_w_down
    gmlp_win = gmlp_w_in.astype(BF16)
    gmlp_wout = gmlp_w_out.astype(BF16)
    hgrn_win = hgrn_w_in.astype(BF16)
    hgrn_wout = hgrn_w_out.astype(BF16)
    attn_wqkv = attn_w_qkv.astype(BF16)
    attn_wo = attn_w_o.astype(BF16)

    h = x.reshape(batch * seq, d)
    for layer in range(depth):
        kind = layer % n_mixers
        slot = layer // n_mixers
        h = _ffn_layer(h, wgu, wd, ln_gain[layer, 0][None], ln_bias[layer, 0][None], layer, 0, alpha)
        gain, bias = ln_gain[layer, 1][None], ln_bias[layer, 1][None]
        if kind == 0:
            group_dim = gmlp_w_out.shape[1] // GMLP_GROUPS
            bs_full = jnp.repeat(jnp.transpose(gmlp_b_spatial[slot]), group_dim, axis=1)
            h = _gmlp_layer(h, gmlp_win, gmlp_ln_gain[slot][None], gmlp_ln_bias[slot][None],
                            gmlp_w_spatial, bs_full, gmlp_wout, gain, bias, slot, alpha)
        elif kind == 1:
            h = _hgrn_layer(h, batch, hgrn_win, hgrn_lb_logits, hgrn_norm_gain[slot][None],
                            hgrn_wout, gain, bias, slot, layer, alpha)
        else:
            h = _swa_layer(h, batch, attn_wqkv, attn_b_qkv[slot][None], attn_sinks[slot],
                           attn_wo, attn_b_o[slot][None], gain, bias, slot, alpha)
        h = _ffn_layer(h, wgu, wd, ln_gain[layer, 2][None], ln_bias[layer, 2][None], layer, 1, alpha)
    return h.reshape(batch, seq, d)
```

```python
import functools
import math

import jax
import jax.numpy as jnp
from jax import lax
from jax.experimental import pallas as pl
from jax.experimental.pallas import tpu as pltpu

F32 = jnp.float32
BF16 = jnp.bfloat16

LN_EPS = 1e-5
RMS_EPS = 1e-6

LANES = 128
SUBLANES = 8
VMEM_LIMIT_BYTES = 56 * 1024 * 1024

FFN_TOKEN_TILE = 1024
OUT_ROW_GROUP = 256
FFN_COL_TILE = 256
GMLP_TOKEN_TILE = 1024
GMLP_ROW_GROUP = 512
HGRN_TOKEN_TILE = 512
HGRN_ROW_GROUP = 256
SWA_TOKEN_TILE = 512
SWA_ROW_GROUP = 512

GMLP_CHUNK = 128
GMLP_GROUPS = 8
HGRN_HEADS = 8
HGRN_CHUNK = 128
ATTN_BLOCK = 128
ATTN_HEAD_DIM = 64
ATTN_Q_HEADS = 16
ATTN_KV_HEADS = 2


def _dot(a, b):
    return jnp.dot(a, b, preferred_element_type=F32)


def _dot_nt(a, b):
    return lax.dot_general(a, b, (((1,), (1,)), ((), ())), preferred_element_type=F32)


def _dot_tn(a, b):
    return lax.dot_general(a, b, (((0,), (0,)), ((), ())), preferred_element_type=F32)


def _layer_norm(y, gain, bias):
    mu = jnp.mean(y, axis=-1, keepdims=True)
    yc = y - mu
    var = jnp.mean(yc * yc, axis=-1, keepdims=True)
    return yc * lax.rsqrt(var + LN_EPS) * gain + bias


def _silu(x):
    return x * jax.nn.sigmoid(x)


def _gelu_erf(x):
    return 0.5 * x * (1.0 + lax.erf(x * math.sqrt(0.5)))


def _project_add_norm(o_ref, x_ref, row0, act, w_ref, b_out, gain_ref, bias_ref, alpha, scale):
    n = act.shape[0]
    rg = min(OUT_ROW_GROUP, n)
    for r in range(n // rg):
        mix = _dot(act[r * rg:(r + 1) * rg, :], w_ref[...])
        if b_out is not None:
            mix = mix + b_out
        if scale != 1.0:
            mix = scale * mix
        rows = slice(row0 + r * rg, row0 + (r + 1) * rg)
        o_ref[rows, :] = _layer_norm(alpha * x_ref[rows, :] + mix, gain_ref[...], bias_ref[...])


def _resident(block_shape, index_map):
    return pl.BlockSpec(block_shape, index_map, pipeline_mode=pl.Buffered(1))


def _compiler_params(n_grid_axes):
    return pltpu.CompilerParams(
        dimension_semantics=("arbitrary",) * n_grid_axes,
        vmem_limit_bytes=VMEM_LIMIT_BYTES,
    )


def _weight_chunk_copies(wgu_hbm, wd_hbm, stage_gu_ref, stage_d_ref, sem_ref, c):
    d_ff = wd_hbm.shape[0]
    ct = FFN_COL_TILE
    slot = c % 2
    return (
        pltpu.make_async_copy(wgu_hbm.at[:, pl.ds(c * ct, ct)], stage_gu_ref.at[slot, 0], sem_ref.at[slot, 0]),
        pltpu.make_async_copy(wgu_hbm.at[:, pl.ds(d_ff + c * ct, ct)], stage_gu_ref.at[slot, 1],
                              sem_ref.at[slot, 1]),
        pltpu.make_async_copy(wd_hbm.at[pl.ds(c * ct, ct), :], stage_d_ref.at[slot], sem_ref.at[slot, 2]),
    )


def _gate_up_chunk(xb, wgu_ref, act_ref, c):
    d_ff = act_ref.shape[1]
    lo = c * FFN_COL_TILE
    gate = _dot(xb, wgu_ref[:, lo:lo + FFN_COL_TILE])
    up = _dot(xb, wgu_ref[:, d_ff + lo:d_ff + lo + FFN_COL_TILE])
    act_ref[:, lo:lo + FFN_COL_TILE] = (_silu(gate) * up).astype(BF16)


def _ffn_kernel(x_ref, wgu_hbm, wd_hbm, gain_ref, bias_ref, o_ref,
                wgu_ref, wd_ref, act_ref, stage_gu_ref, stage_d_ref, sem_ref, *, alpha, layer, half):
    d_ff = wd_ref.shape[0]
    ct = FFN_COL_TILE
    n_chunks = d_ff // ct
    first_step = pl.program_id(0) == 0

    @pl.when(first_step)
    def _():
        def copies(c):
            return _weight_chunk_copies(wgu_hbm.at[layer, half], wd_hbm.at[layer, half],
                                        stage_gu_ref, stage_d_ref, sem_ref, c)

        for cp in copies(0):
            cp.start()
        xb = x_ref[...].astype(BF16)
        for c in range(n_chunks):
            if c + 1 < n_chunks:
                for cp in copies(c + 1):
                    cp.start()
            for cp in copies(c):
                cp.wait()
            slot = c % 2
            wgu_ref[:, c * ct:(c + 1) * ct] = stage_gu_ref[slot, 0].astype(BF16)
            wgu_ref[:, d_ff + c * ct:d_ff + (c + 1) * ct] = stage_gu_ref[slot, 1].astype(BF16)
            wd_ref[c * ct:(c + 1) * ct, :] = stage_d_ref[slot].astype(BF16)
            _gate_up_chunk(xb, wgu_ref, act_ref, c)
        _project_add_norm(o_ref, x_ref, 0, act_ref, wd_ref, None, gain_ref, bias_ref, alpha, 0.5)

    @pl.when(jnp.logical_not(first_step))
    def _():
        xb = x_ref[...].astype(BF16)
        for c in range(n_chunks):
            _gate_up_chunk(xb, wgu_ref, act_ref, c)
        _project_add_norm(o_ref, x_ref, 0, act_ref, wd_ref, None, gain_ref, bias_ref, alpha, 0.5)


def _ffn_layer(h, wgu, wd, gain, bias, layer, half, alpha):
    n, d = h.shape
    d_ff = wd.shape[2]
    tm = min(FFN_TOKEN_TILE, n)
    assert d_ff % FFN_COL_TILE == 0
    return pl.pallas_call(
        functools.partial(_ffn_kernel, alpha=alpha, layer=layer, half=half),
        grid=(n // tm,),
        in_specs=[
            pl.BlockSpec((tm, d), lambda i: (i, 0)),
            pl.BlockSpec(memory_space=pl.ANY),
            pl.BlockSpec(memory_space=pl.ANY),
            _resident((1, d), lambda i: (0, 0)),
            _resident((1, d), lambda i: (0, 0)),
        ],
        out_specs=pl.BlockSpec((tm, d), lambda i: (i, 0)),
        out_shape=jax.ShapeDtypeStruct((n, d), F32),
        scratch_shapes=[
            pltpu.VMEM((d, 2 * d_ff), BF16),
            pltpu.VMEM((d_ff, d), BF16),
            pltpu.VMEM((tm, d_ff), BF16),
            pltpu.VMEM((2, 2, d, FFN_COL_TILE), F32),
            pltpu.VMEM((2, FFN_COL_TILE, d), F32),
            pltpu.SemaphoreType.DMA((2, 3)),
        ],
        compiler_params=_compiler_params(1),
        name=f"ffn_l{layer}_h{half}",
    )(h, wgu, wd, gain, bias)


def _gmlp_kernel(x_ref, win_ref, vg_ref, vb_ref, ws_ref, bs_ref, wout_ref, gain_ref, bias_ref,
                 o_ref, gated_ref, *, alpha):
    tm = x_ref.shape[0]
    width = wout_ref.shape[0]
    gd = width // GMLP_GROUPS
    rg = min(GMLP_ROW_GROUP, tm)
    n_chunks = rg // GMLP_CHUNK

    t_idx = lax.broadcasted_iota(jnp.int32, (GMLP_CHUNK, GMLP_CHUNK), 0)
    s_idx = lax.broadcasted_iota(jnp.int32, (GMLP_CHUNK, GMLP_CHUNK), 1)
    causal = s_idx <= t_idx
    w_s = [jnp.where(causal, ws_ref[g], 0.0).astype(BF16) for g in range(GMLP_GROUPS)]

    for r in range(tm // rg):
        rows = slice(r * rg, (r + 1) * rg)
        xb = x_ref[rows, :].astype(BF16)
        v = _gelu_erf(_dot(xb, win_ref[:, width:]))
        u = _gelu_erf(_dot(xb, win_ref[:, :width]))
        v = _layer_norm(v, vg_ref[...], vb_ref[...]).astype(BF16)
        for g in range(GMLP_GROUPS):
            v_g = jnp.concatenate(
                [v[c * GMLP_CHUNK:(c + 1) * GMLP_CHUNK, g * gd:(g + 1) * gd] for c in range(n_chunks)],
                axis=1)
            mixed = _dot(w_s[g], v_g)
            b_g = bs_ref[:, g * gd:(g + 1) * gd]
            for c in range(n_chunks):
                gated_ref[r * rg + c * GMLP_CHUNK:r * rg + (c + 1) * GMLP_CHUNK, g * gd:(g + 1) * gd] = (
                    mixed[:, c * gd:(c + 1) * gd] + b_g)
        _project_add_norm(o_ref, x_ref, r * rg, (u * gated_ref[rows, :]).astype(BF16), wout_ref, None,
                          gain_ref, bias_ref, alpha, 1.0)


def _gmlp_layer(h, win, vg, vb, ws, bs_full, wout, gain, bias, slot, alpha):
    n, d = h.shape
    width = wout.shape[1]
    tm = min(GMLP_TOKEN_TILE, n)
    return pl.pallas_call(
        functools.partial(_gmlp_kernel, alpha=alpha),
        grid=(n // tm,),
        in_specs=[
            pl.BlockSpec((tm, d), lambda i: (i, 0)),
            _resident((None, d, 2 * width), lambda i: (slot, 0, 0)),
            _resident((1, width), lambda i: (0, 0)),
            _resident((1, width), lambda i: (0, 0)),
            _resident((None, GMLP_GROUPS, GMLP_CHUNK, GMLP_CHUNK), lambda i: (slot, 0, 0, 0)),
            _resident((GMLP_CHUNK, width), lambda i: (0, 0)),
            _resident((None, width, d), lambda i: (slot, 0, 0)),
            _resident((1, d), lambda i: (0, 0)),
            _resident((1, d), lambda i: (0, 0)),
        ],
        out_specs=pl.BlockSpec((tm, d), lambda i: (i, 0)),
        out_shape=jax.ShapeDtypeStruct((n, d), F32),
        scratch_shapes=[pltpu.VMEM((tm, width), F32)],
        compiler_params=_compiler_params(1),
        name=f"gmlp_s{slot}",
    )(h, win, vg, vb, ws, bs_full, wout, gain, bias)


def _cumsum_rows(x):
    n = x.shape[0]
    row = lax.broadcasted_iota(jnp.int32, x.shape, 0)
    shift = 1
    while shift < n:
        if shift % SUBLANES:
            x = x + jnp.where(row >= shift, pltpu.roll(x, shift, axis=0), 0.0)
        else:
            x = jnp.concatenate([x[:shift], x[shift:] + x[:n - shift]], axis=0)
        shift *= 2
    return x


def _hgrn_kernel(x_ref, win_ref, lbl_ref, ng_ref, wout_ref, gain_ref, bias_ref, o_ref,
                 state_ref, *, alpha, layer):
    @pl.when(pl.program_id(1) == 0)
    def _():
        state_ref[...] = jnp.zeros_like(state_ref)

    tm = x_ref.shape[0]
    width = wout_ref.shape[0]
    dk = width // HGRN_HEADS
    rg = min(HGRN_ROW_GROUP, tm)

    logits = lbl_ref[...]
    e = jnp.exp(logits - jnp.max(logits, axis=0, keepdims=True))
    probs = e / jnp.sum(e, axis=0, keepdims=True)
    lb = jnp.zeros((1, width), F32)
    for j in range(1, layer + 1):
        lb = lb + probs[j:j + 1, :]

    row = lax.broadcasted_iota(jnp.int32, (HGRN_CHUNK, HGRN_CHUNK), 0)
    col = lax.broadcasted_iota(jnp.int32, (HGRN_CHUNK, HGRN_CHUNK), 1)
    lower = col <= row
    norm_g = ng_ref[...]

    def chunk_step(q, f, v, gate):
        gc = _cumsum_rows(jnp.log(f))
        g_last = gc[HGRN_CHUNK - 1:HGRN_CHUNK, :]
        g_mid = 0.5 * g_last
        e_mid = jnp.exp(g_mid)
        q_t = q * jnp.exp(gc - g_mid)
        k_t = (1.0 - f) * jnp.exp(g_mid - gc)
        q_d = (q_t * e_mid).astype(BF16)
        k_d = (k_t * e_mid).astype(BF16)
        q_t = q_t.astype(BF16)
        k_t = k_t.astype(BF16)
        v = v.astype(BF16)
        decay = jnp.exp(g_last)
        outs = []
        for h in range(HGRN_HEADS):
            hs = slice(h * dk, (h + 1) * dk)
            scores = jnp.where(lower, _dot_nt(q_t[:, hs], k_t[:, hs]), 0.0)
            st = state_ref[h]
            o_h = _dot(scores.astype(BF16), v[:, hs]) + _dot_nt(q_d[:, hs], st.astype(BF16))
            state_ref[h] = st * decay[:, hs] + _dot_tn(v[:, hs], k_d[:, hs])
            ms = jnp.mean(o_h * o_h, axis=-1, keepdims=True)
            outs.append(o_h * lax.rsqrt(ms + RMS_EPS) * norm_g)
        return (jnp.concatenate(outs, axis=1) * gate).astype(BF16)

    for r in range(tm // rg):
        rows = slice(r * rg, (r + 1) * rg)
        xb = x_ref[rows, :].astype(BF16)
        q = _silu(_dot(xb, win_ref[:, 0:width]))
        f = lb + (1.0 - lb) * jax.nn.sigmoid(_dot(xb, win_ref[:, width:2 * width]))
        v = _dot(xb, win_ref[:, 2 * width:3 * width])
        gate = _silu(_dot(xb, win_ref[:, 3 * width:4 * width]))
        o = jnp.concatenate(
            [chunk_step(q[cs], f[cs], v[cs], gate[cs])
             for cs in (slice(c * HGRN_CHUNK, (c + 1) * HGRN_CHUNK) for c in range(rg // HGRN_CHUNK))],
            axis=0)
        _project_add_norm(o_ref, x_ref, r * rg, o, wout_ref, None, gain_ref, bias_ref, alpha, 1.0)


def _hgrn_layer(h, batch, win, lb_logits, norm_g, wout, gain, bias, slot, layer, alpha):
    n, d = h.shape
    width = wout.shape[1]
    seq = n // batch
    tm = min(HGRN_TOKEN_TILE, seq)
    spb = seq // tm
    depth = lb_logits.shape[0]
    return pl.pallas_call(
        functools.partial(_hgrn_kernel, alpha=alpha, layer=layer),
        grid=(batch, spb),
        in_specs=[
            pl.BlockSpec((tm, d), lambda b, s: (b * spb + s, 0)),
            _resident((None, d, 4 * width), lambda b, s: (slot, 0, 0)),
            _resident((depth, width), lambda b, s: (0, 0)),
            _resident((1, width // HGRN_HEADS), lambda b, s: (0, 0)),
            _resident((None, width, d), lambda b, s: (slot, 0, 0)),
            _resident((1, d), lambda b, s: (0, 0)),
            _resident((1, d), lambda b, s: (0, 0)),
        ],
        out_specs=pl.BlockSpec((tm, d), lambda b, s: (b * spb + s, 0)),
        out_shape=jax.ShapeDtypeStruct((n, d), F32),
        scratch_shapes=[
            pltpu.VMEM((HGRN_HEADS, width // HGRN_HEADS, width // HGRN_HEADS), F32),
        ],
        compiler_params=_compiler_params(2),
        name=f"hgrn_s{slot}",
    )(h, win, lb_logits, norm_g, wout, gain, bias)


def _swa_kernel(x_ref, wqkv_ref, bqkv_ref, sink_ref, wo_ref, bo_ref, gain_ref, bias_ref, o_ref,
                kv_ref, *, alpha):
    L = ATTN_BLOCK
    hd = ATTN_HEAD_DIM
    group = ATTN_Q_HEADS // ATTN_KV_HEADS
    pairs = group // 2
    kvw = ATTN_KV_HEADS * hd
    dq = ATTN_Q_HEADS * hd
    first_tile = pl.program_id(1) == 0

    @pl.when(first_tile)
    def _():
        kv_ref[0:L, :] = jnp.zeros((L, 2 * kvw), F32)

    tm = x_ref.shape[0]
    rg = min(SWA_ROW_GROUP, tm)

    lane =lax.broadcasted_iota(jnp.int32, (2 * L, kvw), 1)
    low = lane < hd
    qi = lax.broadcasted_iota(jnp.int32, (L, 2 * L), 0)
    kj = lax.broadcasted_iota(jnp.int32, (L, 2 * L), 1)
    in_band = (kj > qi) & (kj <= qi + L)
    band_bias = jnp.where(in_band, 0.0, -jnp.inf)
    first_bias = jnp.where(in_band & (kj >= L), 0.0, -jnp.inf)
    ones_cols = jnp.where(low, 1.0, 0.0)
    low_q = lax.broadcasted_iota(jnp.int32, (L, 2 * hd), 1) < hd

    def attend(qb, kv2, bias):
        k2 = kv2[:, 0:kvw]
        v2 = kv2[:, kvw:2 * kvw]
        k2r = pltpu.roll(k2, hd, axis=1)
        v2r = pltpu.roll(v2, hd, axis=1)
        blocks = []
        for hk in range(ATTN_KV_HEADS):
            k_lo, k_hi = (k2, k2r) if hk == 0 else (k2r, k2)
            v_lo, v_hi = (v2, v2r) if hk == 0 else (v2r, v2)
            keys = jnp.concatenate([jnp.where(low, k_lo, 0.0), jnp.where(low, 0.0, k_hi)],
                                   axis=0).astype(BF16)
            vals = jnp.concatenate(
                [jnp.concatenate([jnp.where(low, v_lo, 0.0), ones_cols], axis=1),
                 jnp.concatenate([jnp.where(low, 0.0, v_hi), 1.0 - ones_cols], axis=1)],
                axis=0).astype(BF16)
            q_pairs = jnp.concatenate(
                [qb[:, (hk * pairs + p) * 2 * hd:(hk * pairs + p + 1) * 2 * hd] for p in range(pairs)],
                axis=0)
            s_all = _dot_nt(q_pairs, keys)
            probs, sink_terms = [], []
            for p in range(pairs):
                head = hk * group + 2 * p
                sink_e = sink_ref[head]
                sink_o = sink_ref[head + 1]
                s_e = s_all[p * L:(p + 1) * L, 0:2 * L] + bias
                s_o = s_all[p * L:(p + 1) * L, 2 * L:4 * L] + bias
                m_e = jnp.maximum(jnp.max(s_e, axis=-1, keepdims=True), sink_e)
                m_o = jnp.maximum(jnp.max(s_o, axis=-1, keepdims=True), sink_o)
                probs.append(jnp.concatenate([jnp.exp(s_e - m_e), jnp.exp(s_o - m_o)], axis=1).astype(BF16))
                sink_terms.append(jnp.where(low_q, jnp.exp(sink_e - m_e), jnp.exp(sink_o - m_o)))
            pv = _dot(jnp.concatenate(probs, axis=0), vals)
            for p in range(pairs):
                pv_p = pv[p * L:(p + 1) * L, :]
                blocks.append(pv_p[:, 0:2 * hd] / (pv_p[:, 2 * hd:4 * hd] + sink_terms[p]))
        return jnp.concatenate(blocks, axis=1).astype(BF16)

    for r in range(tm // rg):
        rows = slice(r * rg, (r + 1) * rg)
        xb = x_ref[rows, :].astype(BF16)
        q = ((_dot(xb, wqkv_ref[:, 0:dq]) + bqkv_ref[:, 0:dq]) * (hd ** -0.5)).astype(BF16)
        kv_ref[L + r * rg:L + (r + 1) * rg, :] = (
            _dot(xb, wqkv_ref[:, dq:dq + 2 * kvw]) + bqkv_ref[:, dq:dq + 2 * kvw])
        outs = []
        for nb in range(rg // L):
            n = r * (rg // L) + nb
            bias = jnp.where(first_tile, first_bias, band_bias) if n == 0 else band_bias
            outs.append(attend(q[nb * L:(nb + 1) * L, :], kv_ref[n * L:(n + 2) * L, :], bias))
        _project_add_norm(o_ref, x_ref, r * rg, jnp.concatenate(outs, axis=0), wo_ref, bo_ref[...],
                          gain_ref, bias_ref, alpha, 1.0)

    kv_ref[0:L, :] = kv_ref[tm:tm + L, :]


def _swa_layer(h, batch, wqkv, bqkv, sinks, wo, bo, gain, bias, slot, alpha):
    n, d = h.shape
    seq = n // batch
    tm = min(SWA_TOKEN_TILE, seq)
    spb = seq // tm
    qkv_w = wqkv.shape[2]
    dq = wo.shape[1]
    kvw = ATTN_KV_HEADS * ATTN_HEAD_DIM
    return pl.pallas_call(
        functools.partial(_swa_kernel, alpha=alpha),
        grid=(batch, spb),
        in_specs=[
            pl.BlockSpec((tm, d), lambda b, s: (b * spb + s, 0)),
            _resident((None, d, qkv_w), lambda b, s: (slot, 0, 0)),
            _resident((1, qkv_w), lambda b, s: (0, 0)),
            pl.BlockSpec(memory_space=pltpu.SMEM),
            _resident((None, dq, d), lambda b, s: (slot, 0, 0)),
            _resident((1, d), lambda b, s: (0, 0)),
            _resident((1, d), lambda b, s: (0, 0)),
            _resident((1, d), lambda b, s: (0, 0)),
        ],
        out_specs=pl.BlockSpec((tm, d), lambda b, s: (b * spb + s, 0)),
        out_shape=jax.ShapeDtypeStruct((n, d), F32),
        scratch_shapes=[
            pltpu.VMEM((ATTN_BLOCK + tm, 2 * kvw), F32),
        ],
        compiler_params=_compiler_params(2),
        name=f"swa_s{slot}",
    )(h, wqkv, bqkv, sinks, wo, bo, gain, bias)


def kernel(x, ffn_w_gate_up, ffn_w_down, ln_gain, ln_bias, gmlp_w_in, gmlp_ln_gain, gmlp_ln_bias, gmlp_w_spatial, gmlp_b_spatial, gmlp_w_out, hgrn_w_in, hgrn_lb_logits, hgrn_norm_gain, hgrn_w_out, attn_w_qkv, attn_b_qkv, attn_sinks, attn_w_o, attn_b_o):
    batch, seq, d = x.shape
    depth = ffn_w_gate_up.shape[0]
    n_mixers = 3
    alpha = (2.0 * depth) ** 0.25

    wgu = ffn_w_gate_up
    wd = ffn_w_down
    gmlp_win = gmlp_w_in.astype(BF16)
    gmlp_wout = gmlp_w_out.astype(BF16)
    hgrn_win = hgrn_w_in.astype(BF16)
    hgrn_wout = hgrn_w_out.astype(BF16)
    attn_wqkv = attn_w_qkv.astype(BF16)
    attn_wo = attn_w_o.astype(BF16)

    h = x.reshape(batch * seq, d)
    for layer in range(depth):
        kind = layer % n_mixers
        slot = layer // n_mixers
        h = _ffn_layer(h, wgu, wd, ln_gain[layer, 0][None], ln_bias[layer, 0][None], layer, 0, alpha)
        gain, bias = ln_gain[layer, 1][None], ln_bias[layer, 1][None]
        if kind == 0:
            group_dim = gmlp_w_out.shape[1] // GMLP_GROUPS
            bs_full = jnp.repeat(jnp.transpose(gmlp_b_spatial[slot]), group_dim, axis=1)
            h = _gmlp_layer(h, gmlp_win, gmlp_ln_gain[slot][None], gmlp_ln_bias[slot][None],
                            gmlp_w_spatial, bs_full, gmlp_wout, gain, bias, slot, alpha)
        elif kind == 1:
            h = _hgrn_layer(h, batch, hgrn_win, hgrn_lb_logits, hgrn_norm_gain[slot][None],
                            hgrn_wout, gain, bias, slot, layer, alpha)
        else:
            h = _swa_layer(h, batch, attn_wqkv, attn_b_qkv[slot][None], attn_sinks[slot],
                           attn_wo, attn_b_o[slot][None], gain, bias, slot, alpha)
        h = _ffn_layer(h, wgu, wd, ln_gain[layer, 2][None], ln_bias[layer, 2][None], layer, 1, alpha)
    return h.reshape(batch, seq, d)
```

```python
import functools
import math

import jax
import jax.numpy as jnp
from jax import lax
from jax.experimental import pallas as pl
from jax.experimental.pallas import tpu as pltpu

F32 = jnp.float32
BF16 = jnp.bfloat16

LN_EPS = 1e-5
RMS_EPS = 1e-6

LANES = 128
SUBLANES = 8
VMEM_LIMIT_BYTES = 56 * 1024 * 1024

FFN_TOKEN_TILE = 1024
OUT_ROW_GROUP = 256
FFN_COL_TILE = 256
FFN_WEIGHT_CHUNKS = 8
GMLP_TOKEN_TILE = 1024
GMLP_ROW_GROUP = 512
HGRN_TOKEN_TILE = 512
HGRN_ROW_GROUP = 256
SWA_TOKEN_TILE = 512
SWA_ROW_GROUP = 512

GMLP_CHUNK = 128
GMLP_GROUPS = 8
HGRN_HEADS = 8
HGRN_CHUNK = 128
HGRN_SAFE_LOG_DECAY = 150.0
ATTN_BLOCK = 128
ATTN_HEAD_DIM = 64
ATTN_Q_HEADS = 16
ATTN_KV_HEADS = 2


def _dot(a, b):
    return jnp.dot(a, b, preferred_element_type=F32)


def _dot_nt(a, b):
    return lax.dot_general(a, b, (((1,), (1,)), ((), ())), preferred_element_type=F32)


def _dot_tn(a, b):
    return lax.dot_general(a, b, (((0,), (0,)), ((), ())), preferred_element_type=F32)


def _layer_norm(y, gain, bias):
    mu = jnp.mean(y, axis=-1, keepdims=True)
    yc = y - mu
    var = jnp.mean(yc * yc, axis=-1, keepdims=True)
    return yc * lax.rsqrt(var + LN_EPS) * gain + bias


def _silu(x):
    return x * jax.nn.sigmoid(x)


def _gelu_erf(x):
    return 0.5 * x * (1.0 + lax.erf(x * math.sqrt(0.5)))


def _project_add_norm(o_ref, x_ref, row0, act, w_ref, b_out, gain_ref, bias_ref, alpha, scale):
    n = act.shape[0]
    rg = min(OUT_ROW_GROUP, n)
    for r in range(n // rg):
        mix = _dot(act[r * rg:(r + 1) * rg, :], w_ref[...])
        if b_out is not None:
            mix = mix + b_out
        if scale != 1.0:
            mix = scale * mix
        rows = slice(row0 + r * rg, row0 + (r + 1) * rg)
        o_ref[rows, :] = _layer_norm(alpha * x_ref[rows, :] + mix, gain_ref[...], bias_ref[...])


def _resident(block_shape, index_map):
    return pl.BlockSpec(block_shape, index_map, pipeline_mode=pl.Buffered(1))


def _compiler_params(n_grid_axes):
    return pltpu.CompilerParams(
        dimension_semantics=("arbitrary",) * n_grid_axes,
        vmem_limit_bytes=VMEM_LIMIT_BYTES,
    )


def _weight_chunk_copy(w_hbm, stage_ref, sem_ref, c):
    rows = stage_ref.shape[1]
    return pltpu.make_async_copy(w_hbm.at[pl.ds(c * rows, rows), :], stage_ref.at[c % 2], sem_ref.at[c % 2])


def _load_weights_as_bf16(w_hbm, w_ref, stage_ref, sem_ref):
    rows = stage_ref.shape[1]
    n_chunks = w_ref.shape[0] // rows
    _weight_chunk_copy(w_hbm, stage_ref, sem_ref, 0).start()
    for c in range(n_chunks):
        if c + 1 < n_chunks:
            _weight_chunk_copy(w_hbm, stage_ref, sem_ref, c + 1).start()
        _weight_chunk_copy(w_hbm, stage_ref, sem_ref, c).wait()
        w_ref[c * rows:(c + 1) * rows, :] = stage_ref[c % 2].astype(BF16)


def _ffn_kernel(x_ref, wgu_hbm, wd_hbm, gain_ref, bias_ref, o_ref,
                wgu_ref, wd_ref, act_ref, stage_gu_ref, stage_d_ref, sem_gu_ref, sem_d_ref,
                *, alpha, layer, half):
    @pl.when(pl.program_id(0) == 0)
    def _():
        _load_weights_as_bf16(wgu_hbm.at[layer, half], wgu_ref, stage_gu_ref, sem_gu_ref)
        _load_weights_as_bf16(wd_hbm.at[layer, half], wd_ref, stage_d_ref, sem_d_ref)

    xb = x_ref[...].astype(BF16)
    d_ff = wd_ref.shape[0]
    for c in range(d_ff // FFN_COL_TILE):
        lo = c * FFN_COL_TILE
        gate = _dot(xb, wgu_ref[:, lo:lo + FFN_COL_TILE])
        up = _dot(xb, wgu_ref[:, d_ff + lo:d_ff + lo + FFN_COL_TILE])
        act_ref[:, lo:lo + FFN_COL_TILE] = (_silu(gate) * up).astype(BF16)
    _project_add_norm(o_ref, x_ref, 0, act_ref, wd_ref, None, gain_ref, bias_ref, alpha, 0.5)


def _ffn_layer(h, wgu, wd, gain, bias, layer, half, alpha):
    n, d = h.shape
    d_ff = wd.shape[2]
    tm = min(FFN_TOKEN_TILE, n)
    return pl.pallas_call(
        functools.partial(_ffn_kernel, alpha=alpha, layer=layer, half=half),
        grid=(n // tm,),
        in_specs=[
            pl.BlockSpec((tm, d), lambda i: (i, 0)),
            pl.BlockSpec(memory_space=pl.ANY),
            pl.BlockSpec(memory_space=pl.ANY),
            _resident((1, d), lambda i: (0, 0)),
            _resident((1, d), lambda i: (0, 0)),
        ],
        out_specs=pl.BlockSpec((tm, d), lambda i: (i, 0)),
        out_shape=jax.ShapeDtypeStruct((n, d), F32),
        scratch_shapes=[
            pltpu.VMEM((d, 2 * d_ff), BF16),
            pltpu.VMEM((d_ff, d), BF16),
            pltpu.VMEM((tm, d_ff), BF16),
            pltpu.VMEM((2, d // FFN_WEIGHT_CHUNKS, 2 * d_ff), F32),
            pltpu.VMEM((2, d_ff // FFN_WEIGHT_CHUNKS, d), F32),
            pltpu.SemaphoreType.DMA((2,)),
            pltpu.SemaphoreType.DMA((2,)),
        ],
        compiler_params=_compiler_params(1),
        name=f"ffn_l{layer}_h{half}",
    )(h, wgu, wd, gain, bias)


def _gmlp_kernel(x_ref, win_ref, vg_ref, vb_ref, ws_ref, bs_ref, wout_ref, gain_ref, bias_ref,
                 o_ref, gated_ref, *, alpha):
    tm = x_ref.shape[0]
    width = wout_ref.shape[0]
    gd = width // GMLP_GROUPS
    rg = min(GMLP_ROW_GROUP, tm)
    n_chunks = rg // GMLP_CHUNK

    t_idx = lax.broadcasted_iota(jnp.int32, (GMLP_CHUNK, GMLP_CHUNK), 0)
    s_idx = lax.broadcasted_iota(jnp.int32, (GMLP_CHUNK, GMLP_CHUNK), 1)
    causal = s_idx <= t_idx
    w_s = [jnp.where(causal, ws_ref[g], 0.0).astype(BF16) for g in range(GMLP_GROUPS)]

    for r in range(tm // rg):
        rows = slice(r * rg, (r + 1) * rg)
        xb = x_ref[rows, :].astype(BF16)
        v = _gelu_erf(_dot(xb, win_ref[:, width:]))
        u = _gelu_erf(_dot(xb, win_ref[:, :width]))
        v = _layer_norm(v, vg_ref[...], vb_ref[...]).astype(BF16)
        for g in range(GMLP_GROUPS):
            v_g = jnp.concatenate(
                [v[c * GMLP_CHUNK:(c + 1) * GMLP_CHUNK, g * gd:(g + 1) * gd] for c in range(n_chunks)],
                axis=1)
            mixed = _dot(w_s[g], v_g)
            b_g = bs_ref[:, g * gd:(g + 1) * gd]
            for c in range(n_chunks):
                gated_ref[r * rg + c * GMLP_CHUNK:r * rg + (c + 1) * GMLP_CHUNK, g * gd:(g + 1) * gd] = (
                    mixed[:, c * gd:(c + 1) * gd] + b_g)
        _project_add_norm(o_ref, x_ref, r * rg, (u * gated_ref[rows, :]).astype(BF16), wout_ref, None,
                          gain_ref, bias_ref, alpha, 1.0)


def _gmlp_layer(h, win, vg, vb, ws, bs_full, wout, gain, bias, slot, alpha):
    n, d = h.shape
    width = wout.shape[1]
    tm = min(GMLP_TOKEN_TILE, n)
    return pl.pallas_call(
        functools.partial(_gmlp_kernel, alpha=alpha),
        grid=(n // tm,),
        in_specs=[
            pl.BlockSpec((tm, d), lambda i: (i, 0)),
            _resident((None, d, 2 * width), lambda i: (slot, 0, 0)),
            _resident((1, width), lambda i: (0, 0)),
            _resident((1, width), lambda i: (0, 0)),
            _resident((None, GMLP_GROUPS, GMLP_CHUNK, GMLP_CHUNK), lambda i: (slot, 0, 0, 0)),
            _resident((GMLP_CHUNK, width), lambda i: (0, 0)),
            _resident((None, width, d), lambda i: (slot, 0, 0)),
            _resident((1, d), lambda i: (0, 0)),
            _resident((1, d), lambda i: (0, 0)),
        ],
        out_specs=pl.BlockSpec((tm, d), lambda i: (i, 0)),
        out_shape=jax.ShapeDtypeStruct((n, d), F32),
        scratch_shapes=[pltpu.VMEM((tm, width), F32)],
        compiler_params=_compiler_params(1),
        name=f"gmlp_s{slot}",
    )(h, win, vg, vb, ws, bs_full, wout, gain, bias)


def _cumsum_rows(x):
    n = x.shape[0]
    row = lax.broadcasted_iota(jnp.int32, x.shape, 0)
    shift = 1
    while shift < n:
        if shift % SUBLANES:
            x = x + jnp.where(row >= shift, pltpu.roll(x, shift, axis=0), 0.0)
        else:
            x = jnp.concatenate([x[:shift], x[shift:] + x[:n - shift]], axis=0)
        shift *= 2
    return x


def _hgrn_projections(xb, win_ref, lb, width):
    q = _silu(_dot(xb, win_ref[:, 0:width]))
    f = lb + (1.0 - lb) * jax.nn.sigmoid(_dot(xb, win_ref[:, width:2 * width]))
    v = _dot(xb, win_ref[:, 2 * width:3 * width])
    gate = _silu(_dot(xb, win_ref[:, 3 * width:4 * width]))
    return q, f, v, gate


def _hgrn_kernel(x_ref, win_ref, lbl_ref, ng_ref, wout_ref, gain_ref, bias_ref, o_ref,
                 state_ref, state0_ref, q_ref, k_ref, v_ref, gc_ref, acc_ref, *, alpha, layer):
    @pl.when(pl.program_id(1) == 0)
    def _():
        state_ref[...] = jnp.zeros_like(state_ref)

    state0_ref[...] = state_ref[...]

    tm = x_ref.shape[0]
    width = wout_ref.shape[0]
    dk = width // HGRN_HEADS
    rg = min(HGRN_ROW_GROUP, tm)
    C = HGRN_CHUNK

    logits = lbl_ref[...]
    e = jnp.exp(logits - jnp.max(logits, axis=0, keepdims=True))
    probs = e / jnp.sum(e, axis=0, keepdims=True)
    lb = jnp.zeros((1, width), F32)
    for j in range(1, layer + 1):
        lb = lb + probs[j:j + 1, :]

    row = lax.broadcasted_iota(jnp.int32, (C, C), 0)
    col = lax.broadcasted_iota(jnp.int32, (C, C), 1)
    lower = col <= row
    norm_g = ng_ref[...]

    def finish_head(h, o_intra, q_d, k_d, v, decay):
        hs = slice(h * dk, (h + 1) * dk)
        st = state_ref[h]
        o_h = o_intra + _dot_nt(q_d[:, hs], st.astype(BF16))
        state_ref[h] = st * decay[:, hs] + _dot_tn(v[:, hs], k_d[:, hs])
        ms = jnp.mean(o_h * o_h, axis=-1, keepdims=True)
        return o_h * lax.rsqrt(ms + RMS_EPS) * norm_g

    def chunk_step(q, f, v, gate):
        gc = _cumsum_rows(jnp.log(f))
        g_last = gc[C - 1:C, :]
        g_mid = 0.5 * g_last
        e_mid = jnp.exp(g_mid)
        q_t = q * jnp.exp(gc - g_mid)
        k_t = (1.0 - f) * jnp.exp(g_mid - gc)
        q_d = (q_t * e_mid).astype(BF16)
        k_d = (k_t * e_mid).astype(BF16)
        q_t = q_t.astype(BF16)
        k_t = k_t.astype(BF16)
        v = v.astype(BF16)
        decay = jnp.exp(g_last)
        outs = []
        for h in range(HGRN_HEADS):
            hs = slice(h * dk, (h + 1) * dk)
            scores = jnp.where(lower, _dot_nt(q_t[:, hs], k_t[:, hs]), 0.0)
            outs.append(finish_head(h, _dot(scores.astype(BF16), v[:, hs]), q_d, k_d, v, decay))
        return (jnp.concatenate(outs, axis=1) * gate).astype(BF16), jnp.abs(g_last)

    worst = jnp.zeros((1, width), F32)
    for r in range(tm // rg):
        rows = slice(r * rg, (r + 1) * rg)
        q, f, v, gate = _hgrn_projections(x_ref[rows, :].astype(BF16), win_ref, lb, width)
        outs = []
        for c in range(rg // C):
            cs = slice(c * C, (c + 1) * C)
            out, g_abs = chunk_step(q[cs], f[cs], v[cs], gate[cs])
            outs.append(out)
            worst = jnp.maximum(worst, g_abs)
        _project_add_norm(o_ref, x_ref, r * rg, jnp.concatenate(outs, axis=0), wout_ref, None,
                          gain_ref, bias_ref, alpha, 1.0)

    @pl.when(jnp.logical_not(jnp.max(worst) <= HGRN_SAFE_LOG_DECAY))
    def _():
        state_ref[...] = state0_ref[...]
        row_w = lax.broadcasted_iota(jnp.int32, (C, width), 0)

        def exact_chunk(c, carry):
            rows = pl.ds(pl.multiple_of(c * C, C), C)
            x = x_ref[rows, :]
            q, f, v, gate = _hgrn_projections(x.astype(BF16), win_ref, lb, width)
            gc = _cumsum_rows(jnp.log(f))
            g_last = gc[C - 1:C, :]
            q_ref[...] = q
            k_ref[...] = 1.0 - f
            v_ref[...] = v
            gc_ref[...] = gc
            acc_ref[...] = jnp.zeros_like(acc_ref)

            def add_key(s, carry):
                sel = pl.ds(s, 1)
                decay = jnp.exp(jnp.minimum(gc_ref[...] - gc_ref[sel, :], 0.0))
                w = jnp.where(row_w >= s, q_ref[...] * k_ref[sel, :] * decay, 0.0)
                v_s = v_ref[sel, :]
                for h in range(HGRN_HEADS):
                    hs = slice(h * dk, (h + 1) * dk)
                    acc_ref[:, hs] += jnp.sum(w[:, hs], axis=-1, keepdims=True) * v_s[:, hs]
                return carry

            lax.fori_loop(0, C, add_key, 0)
            q_d = (q * jnp.exp(gc)).astype(BF16)
            k_d = ((1.0 - f) * jnp.exp(g_last - gc)).astype(BF16)
            vb = v.astype(BF16)
            decay = jnp.exp(g_last)
            outs = [finish_head(h, acc_ref[:, h * dk:(h + 1) * dk], q_d, k_d, vb, decay)
                    for h in range(HGRN_HEADS)]
            out = (jnp.concatenate(outs, axis=1) * gate).astype(BF16)
            mix = _dot(out, wout_ref[...])
            o_ref[rows, :] = _layer_norm(alpha * x + mix, gain_ref[...], bias_ref[...])
            return carry

        lax.fori_loop(0, tm // C, exact_chunk, 0)


def _hgrn_layer(h, batch, win, lb_logits, norm_g, wout, gain, bias, slot, layer, alpha):
    n, d = h.shape
    width = wout.shape[1]
    seq = n // batch
    tm = min(HGRN_TOKEN_TILE, seq)
    spb = seq // tm
    depth = lb_logits.shape[0]
    return pl.pallas_call(
        functools.partial(_hgrn_kernel, alpha=alpha, layer=layer),
        grid=(batch, spb),
        in_specs=[
            pl.BlockSpec((tm, d), lambda b, s: (b * spb + s, 0)),
            _resident((None, d, 4 * width), lambda b, s: (slot, 0, 0)),
            _resident((depth, width), lambda b, s: (0, 0)),
            _resident((1, width // HGRN_HEADS), lambda b, s: (0, 0)),
            _resident((None, width, d), lambda b, s: (slot, 0, 0)),
            _resident((1, d), lambda b, s: (0, 0)),
            _resident((1, d), lambda b, s: (0, 0)),
        ],
        out_specs=pl.BlockSpec((tm, d), lambda b, s: (b * spb + s, 0)),
        out_shape=jax.ShapeDtypeStruct((n, d), F32),
        scratch_shapes=[
            pltpu.VMEM((HGRN_HEADS, width // HGRN_HEADS, width // HGRN_HEADS), F32),
            pltpu.VMEM((HGRN_HEADS, width // HGRN_HEADS, width // HGRN_HEADS), F32),
            pltpu.VMEM((HGRN_CHUNK, width), F32),
            pltpu.VMEM((HGRN_CHUNK, width), F32),
            pltpu.VMEM((HGRN_CHUNK, width), F32),
            pltpu.VMEM((HGRN_CHUNK, width), F32),
            pltpu.VMEM((HGRN_CHUNK, width), F32),
        ],
        compiler_params=_compiler_params(2),
        name=f"hgrn_s{slot}",
    )(h, win, lb_logits, norm_g, wout, gain, bias)


def _swa_kernel(x_ref, wqkv_ref, bqkv_ref, sink_ref, wo_ref, bo_ref, gain_ref, bias_ref, o_ref,
                kv_ref, *, alpha):
    L = ATTN_BLOCK
    hd = ATTN_HEAD_DIM
    group = ATTN_Q_HEADS // ATTN_KV_HEADS
    pairs = group // 2
    kvw = ATTN_KV_HEADS * hd
    dq = ATTN_Q_HEADS * hd
    first_tile = pl.program_id(1) == 0

    @pl.when(first_tile)
    def _():
        kv_ref[0:L, :] = jnp.zeros((L, 2 * kvw), F32)

    tm = x_ref.shape[0]
    rg = min(SWA_ROW_GROUP, tm)

    lane =lax.broadcasted_iota(jnp.int32, (2 * L, kvw), 1)
    low = lane < hd
    qi = lax.broadcasted_iota(jnp.int32, (L, 2 * L), 0)
    kj = lax.broadcasted_iota(jnp.int32, (L, 2 * L), 1)
    in_band = (kj > qi) & (kj <= qi + L)
    band_bias = jnp.where(in_band, 0.0, -jnp.inf)
    first_bias = jnp.where(in_band & (kj >= L), 0.0, -jnp.inf)
    ones_cols = jnp.where(low, 1.0, 0.0)
    low_q = lax.broadcasted_iota(jnp.int32, (L, 2 * hd), 1) < hd

    def attend(qb, kv2, bias):
        k2 = kv2[:, 0:kvw]
        v2 = kv2[:, kvw:2 * kvw]
        k2r = pltpu.roll(k2, hd, axis=1)
        v2r = pltpu.roll(v2, hd, axis=1)
        blocks = []
        for hk in range(ATTN_KV_HEADS):
            k_lo, k_hi = (k2, k2r) if hk == 0 else (k2r, k2)
            v_lo, v_hi = (v2, v2r) if hk == 0 else (v2r, v2)
            keys = jnp.concatenate([jnp.where(low, k_lo, 0.0), jnp.where(low, 0.0, k_hi)],
                                   axis=0).astype(BF16)
            vals = jnp.concatenate(
                [jnp.concatenate([jnp.where(low, v_lo, 0.0), ones_cols], axis=1),
                 jnp.concatenate([jnp.where(low, 0.0, v_hi), 1.0 - ones_cols], axis=1)],
                axis=0).astype(BF16)
            q_pairs = jnp.concatenate(
                [qb[:, (hk * pairs + p) * 2 * hd:(hk * pairs + p + 1) * 2 * hd] for p in range(pairs)],
                axis=0)
            s_all = _dot_nt(q_pairs, keys)
            probs, sink_terms = [], []
            for p in range(pairs):
                head = hk * group + 2 * p
                sink_e = sink_ref[head]
                sink_o = sink_ref[head + 1]
                s_e = s_all[p * L:(p + 1) * L, 0:2 * L] + bias
                s_o = s_all[p * L:(p + 1) * L, 2 * L:4 * L] + bias
                m_e = jnp.maximum(jnp.max(s_e, axis=-1, keepdims=True), sink_e)
                m_o = jnp.maximum(jnp.max(s_o, axis=-1, keepdims=True), sink_o)
                probs.append(jnp.concatenate([jnp.exp(s_e - m_e), jnp.exp(s_o - m_o)], axis=1).astype(BF16))
                sink_terms.append(jnp.where(low_q, jnp.exp(sink_e - m_e), jnp.exp(sink_o - m_o)))
            pv = _dot(jnp.concatenate(probs, axis=0), vals)
            for p in range(pairs):
                pv_p = pv[p * L:(p + 1) * L, :]
                blocks.append(pv_p[:, 0:2 * hd] / (pv_p[:, 2 * hd:4 * hd] + sink_terms[p]))
        return jnp.concatenate(blocks, axis=1).astype(BF16)

    for r in range(tm // rg):
        rows = slice(r * rg, (r + 1) * rg)
        xb = x_ref[rows, :].astype(BF16)
        q = ((_dot(xb, wqkv_ref[:, 0:dq]) + bqkv_ref[:, 0:dq]) * (hd ** -0.5)).astype(BF16)
        kv_ref[L + r * rg:L + (r + 1) * rg, :] = (
            _dot(xb, wqkv_ref[:, dq:dq + 2 * kvw]) + bqkv_ref[:, dq:dq + 2 * kvw])
        outs = []
        for nb in range(rg // L):
            n = r * (rg // L) + nb
            bias = jnp.where(first_tile, first_bias, band_bias) if n == 0 else band_bias
            outs.append(attend(q[nb * L:(nb + 1) * L, :], kv_ref[n * L:(n + 2) * L, :], bias))
        _project_add_norm(o_ref, x_ref, r * rg, jnp.concatenate(outs, axis=0), wo_ref, bo_ref[...],
                          gain_ref, bias_ref, alpha, 1.0)

    kv_ref[0:L, :] = kv_ref[tm:tm + L, :]


def _swa_layer(h, batch, wqkv, bqkv, sinks, wo, bo, gain, bias, slot, alpha):
    n, d = h.shape
    seq = n // batch
    tm = min(SWA_TOKEN_TILE, seq)
    spb = seq // tm
    qkv_w = wqkv.shape[2]
    dq = wo.shape[1]
    kvw = ATTN_KV_HEADS * ATTN_HEAD_DIM
    return pl.pallas_call(
        functools.partial(_swa_kernel, alpha=alpha),
        grid=(batch, spb),
        in_specs=[
            pl.BlockSpec((tm, d), lambda b, s: (b * spb + s, 0)),
            _resident((None, d, qkv_w), lambda b, s: (slot, 0, 0)),
            _resident((1, qkv_w), lambda b, s: (0, 0)),
            pl.BlockSpec(memory_space=pltpu.SMEM),
            _resident((None, dq, d), lambda b, s: (slot, 0, 0)),
            _resident((1, d), lambda b, s: (0, 0)),
            _resident((1, d), lambda b, s: (0, 0)),
            _resident((1, d), lambda b, s: (0, 0)),
        ],
        out_specs=pl.BlockSpec((tm, d), lambda b, s: (b * spb + s, 0)),
        out_shape=jax.ShapeDtypeStruct((n, d), F32),
        scratch_shapes=[
            pltpu.VMEM((ATTN_BLOCK + tm, 2 * kvw), F32),
        ],
        compiler_params=_compiler_params(2),
        name=f"swa_s{slot}",
    )(h, wqkv, bqkv, sinks, wo, bo, gain, bias)


def kernel(x, ffn_w_gate_up, ffn_w_down, ln_gain, ln_bias, gmlp_w_in, gmlp_ln_gain, gmlp_ln_bias, gmlp_w_spatial, gmlp_b_spatial, gmlp_w_out, hgrn_w_in, hgrn_lb_logits, hgrn_norm_gain, hgrn_w_out, attn_w_qkv, attn_b_qkv, attn_sinks, attn_w_o, attn_b_o):
    batch, seq, d = x.shape
    depth = ffn_w_gate_up.shape[0]
    n_mixers = 3
    alpha = (2.0 * depth) ** 0.25

    wgu = ffn_w_gate_up
    wd = ffn_w_down
    gmlp_win = gmlp_w_in.astype(BF16)
    gmlp_wout = gmlp_w_out.astype(BF16)
    hgrn_win = hgrn_w_in.astype(BF16)
    hgrn_wout = hgrn_w_out.astype(BF16)
    attn_wqkv = attn_w_qkv.astype(BF16)
    attn_wo = attn_w_o.astype(BF16)

    h = x.reshape(batch * seq, d)
    for layer in range(depth):
        kind = layer % n_mixers
        slot = layer // n_mixers
        h = _ffn_layer(h, wgu, wd, ln_gain[layer, 0][None], ln_bias[layer, 0][None], layer, 0, alpha)
        gain, bias = ln_gain[layer, 1][None], ln_bias[layer, 1][None]
        if kind == 0:
            group_dim = gmlp_w_out.shape[1] // GMLP_GROUPS
            bs_full = jnp.repeat(jnp.transpose(gmlp_b_spatial[slot]), group_dim, axis=1)
            h = _gmlp_layer(h, gmlp_win, gmlp_ln_gain[slot][None], gmlp_ln_bias[slot][None],
                            gmlp_w_spatial, bs_full, gmlp_wout, gain, bias, slot, alpha)
        elif kind == 1:
            h = _hgrn_layer(h, batch, hgrn_win, hgrn_lb_logits, hgrn_norm_gain[slot][None],
                            hgrn_wout, gain, bias, slot, layer, alpha)
        else:
            h = _swa_layer(h, batch, attn_wqkv, attn_b_qkv[slot][None], attn_sinks[slot],
                           attn_wo, attn_b_o[slot][None], gain, bias, slot, alpha)
        h = _ffn_layer(h, wgu, wd, ln_gain[layer, 2][None], ln_bias[layer, 2][None], layer, 1, alpha)
    return h.reshape(batch, seq, d)
```

```python
import functools
import math

import jax
import jax.numpy as jnp
from jax import lax
from jax.experimental import pallas as pl
from jax.experimental.pallas import tpu as pltpu

F32 = jnp.float32
BF16 = jnp.bfloat16

LN_EPS = 1e-5
RMS_EPS = 1e-6

LANES = 128
SUBLANES = 8
VMEM_LIMIT_BYTES = 56 * 1024 * 1024

FFN_TOKEN_TILE = 1024
OUT_ROW_GROUP = 256
FFN_COL_TILE = 256
FFN_WEIGHT_CHUNKS = 8
GMLP_TOKEN_TILE = 1024
GMLP_ROW_GROUP = 512
HGRN_TOKEN_TILE = 512
HGRN_ROW_GROUP = 256
SWA_TOKEN_TILE = 512
SWA_ROW_GROUP = 512

GMLP_CHUNK = 128
GMLP_GROUPS = 8
HGRN_HEADS = 8
HGRN_CHUNK = 128
HGRN_SAFE_LOG_DECAY = 150.0
ATTN_BLOCK = 128
ATTN_HEAD_DIM = 64
ATTN_Q_HEADS = 16
ATTN_KV_HEADS = 2


def _dot(a, b):
    return jnp.dot(a, b, preferred_element_type=F32)


def _dot_nt(a, b):
    return lax.dot_general(a, b, (((1,), (1,)), ((), ())), preferred_element_type=F32)


def _dot_tn(a, b):
    return lax.dot_general(a, b, (((0,), (0,)), ((), ())), preferred_element_type=F32)


def _layer_norm(y, gain, bias):
    mu = jnp.mean(y, axis=-1, keepdims=True)
    yc = y - mu
    var = jnp.mean(yc * yc, axis=-1, keepdims=True)
    return yc * lax.rsqrt(var + LN_EPS) * gain + bias


def _silu(x):
    return x * jax.nn.sigmoid(x)


def _gelu_erf(x):
    return 0.5 * x * (1.0 + lax.erf(x * math.sqrt(0.5)))


def _project_add_norm(o_ref, x_ref, row0, act, w_ref, b_out, gain_ref, bias_ref, alpha, scale):
    n = act.shape[0]
    rg = min(OUT_ROW_GROUP, n)
    for r in range(n // rg):
        mix = _dot(act[r * rg:(r + 1) * rg, :], w_ref[...])
        if b_out is not None:
            mix = mix + b_out
        if scale != 1.0:
            mix = scale * mix
        rows = slice(row0 + r * rg, row0 + (r + 1) * rg)
        o_ref[rows, :] = _layer_norm(alpha * x_ref[rows, :] + mix, gain_ref[...], bias_ref[...])


def _resident(block_shape, index_map):
    return pl.BlockSpec(block_shape, index_map, pipeline_mode=pl.Buffered(1))


def _compiler_params(n_grid_axes):
    return pltpu.CompilerParams(
        dimension_semantics=("arbitrary",) * n_grid_axes,
        vmem_limit_bytes=VMEM_LIMIT_BYTES,
    )


def _weight_chunk_copy(w_hbm, stage_ref, sem_ref, c):
    rows = stage_ref.shape[1]
    return pltpu.make_async_copy(w_hbm.at[pl.ds(c * rows, rows), :], stage_ref.at[c % 2], sem_ref.at[c % 2])


def _load_weights_as_bf16(streams):
    n_chunks = {w_ref.shape[0] // stage_ref.shape[1] for _, w_ref, stage_ref, _ in streams}
    (n_chunks,) = n_chunks
    for w_hbm, _, stage_ref, sem_ref in streams:
        _weight_chunk_copy(w_hbm, stage_ref, sem_ref, 0).start()
    for c in range(n_chunks):
        for w_hbm, w_ref, stage_ref, sem_ref in streams:
            rows = stage_ref.shape[1]
            if c + 1 < n_chunks:
                _weight_chunk_copy(w_hbm, stage_ref, sem_ref, c + 1).start()
            _weight_chunk_copy(w_hbm, stage_ref, sem_ref, c).wait()
            w_ref[c * rows:(c + 1) * rows, :] = stage_ref[c % 2].astype(BF16)


def _ffn_kernel(x_ref, wgu_hbm, wd_hbm, gain_ref, bias_ref, o_ref,
                wgu_ref, wd_ref, act_ref, stage_gu_ref, stage_d_ref, sem_gu_ref, sem_d_ref,
                *, alpha, layer, half):
    @pl.when(pl.program_id(0) == 0)
    def _():
        _load_weights_as_bf16([(wgu_hbm.at[layer, half], wgu_ref, stage_gu_ref, sem_gu_ref),
                               (wd_hbm.at[layer, half], wd_ref, stage_d_ref, sem_d_ref)])

    xb = x_ref[...].astype(BF16)
    d_ff = wd_ref.shape[0]
    for c in range(d_ff // FFN_COL_TILE):
        lo = c * FFN_COL_TILE
        gate = _dot(xb, wgu_ref[:, lo:lo + FFN_COL_TILE])
        up = _dot(xb, wgu_ref[:, d_ff + lo:d_ff + lo + FFN_COL_TILE])
        act_ref[:, lo:lo + FFN_COL_TILE] = (_silu(gate) * up).astype(BF16)
    _project_add_norm(o_ref, x_ref, 0, act_ref, wd_ref, None, gain_ref, bias_ref, alpha, 0.5)


def _ffn_layer(h, wgu, wd, gain, bias, layer, half, alpha):
    n, d = h.shape
    d_ff = wd.shape[2]
    tm = min(FFN_TOKEN_TILE, n)
    return pl.pallas_call(
        functools.partial(_ffn_kernel, alpha=alpha, layer=layer, half=half),
        grid=(n // tm,),
        in_specs=[
            pl.BlockSpec((tm, d), lambda i: (i, 0)),
            pl.BlockSpec(memory_space=pl.ANY),
            pl.BlockSpec(memory_space=pl.ANY),
            _resident((1, d), lambda i: (0, 0)),
            _resident((1, d), lambda i: (0, 0)),
        ],
        out_specs=pl.BlockSpec((tm, d), lambda i: (i, 0)),
        out_shape=jax.ShapeDtypeStruct((n, d), F32),
        scratch_shapes=[
            pltpu.VMEM((d, 2 * d_ff), BF16),
            pltpu.VMEM((d_ff, d), BF16),
            pltpu.VMEM((tm, d_ff), BF16),
            pltpu.VMEM((2, d // FFN_WEIGHT_CHUNKS, 2 * d_ff), F32),
            pltpu.VMEM((2, d_ff // FFN_WEIGHT_CHUNKS, d), F32),
            pltpu.SemaphoreType.DMA((2,)),
            pltpu.SemaphoreType.DMA((2,)),
        ],
        compiler_params=_compiler_params(1),
        name=f"ffn_l{layer}_h{half}",
    )(h, wgu, wd, gain, bias)


def _gmlp_kernel(x_ref, win_ref, vg_ref, vb_ref, ws_ref, bs_ref, wout_ref, gain_ref, bias_ref,
                 o_ref, gated_ref, *, alpha):
    tm = x_ref.shape[0]
    width = wout_ref.shape[0]
    gd = width // GMLP_GROUPS
    rg = min(GMLP_ROW_GROUP, tm)
    n_chunks = rg // GMLP_CHUNK

    t_idx = lax.broadcasted_iota(jnp.int32, (GMLP_CHUNK, GMLP_CHUNK), 0)
    s_idx = lax.broadcasted_iota(jnp.int32, (GMLP_CHUNK, GMLP_CHUNK), 1)
    causal = s_idx <= t_idx
    w_s = [jnp.where(causal, ws_ref[g], 0.0).astype(BF16) for g in range(GMLP_GROUPS)]

    for r in range(tm // rg):
        rows = slice(r * rg, (r + 1) * rg)
        xb = x_ref[rows, :].astype(BF16)
        v = _gelu_erf(_dot(xb, win_ref[:, width:]))
        u = _gelu_erf(_dot(xb, win_ref[:, :width]))
        v = _layer_norm(v, vg_ref[...], vb_ref[...]).astype(BF16)
        for g in range(GMLP_GROUPS):
            v_g = jnp.concatenate(
                [v[c * GMLP_CHUNK:(c + 1) * GMLP_CHUNK, g * gd:(g + 1) * gd] for c in range(n_chunks)],
                axis=1)
            mixed = _dot(w_s[g], v_g)
            b_g = bs_ref[:, g * gd:(g + 1) * gd]
            for c in range(n_chunks):
                gated_ref[r * rg + c * GMLP_CHUNK:r * rg + (c + 1) * GMLP_CHUNK, g * gd:(g + 1) * gd] = (
                    mixed[:, c * gd:(c + 1) * gd] + b_g)
        _project_add_norm(o_ref, x_ref, r * rg, (u * gated_ref[rows, :]).astype(BF16), wout_ref, None,
                          gain_ref, bias_ref, alpha, 1.0)


def _gmlp_layer(h, win, vg, vb, ws, bs_full, wout, gain, bias, slot, alpha):
    n, d = h.shape
    width = wout.shape[1]
    tm = min(GMLP_TOKEN_TILE, n)
    return pl.pallas_call(
        functools.partial(_gmlp_kernel, alpha=alpha),
        grid=(n // tm,),
        in_specs=[
            pl.BlockSpec((tm, d), lambda i: (i, 0)),
            _resident((None, d, 2 * width), lambda i: (slot, 0, 0)),
            _resident((1, width), lambda i: (0, 0)),
            _resident((1, width), lambda i: (0, 0)),
            _resident((None, GMLP_GROUPS, GMLP_CHUNK, GMLP_CHUNK), lambda i: (slot, 0, 0, 0)),
            _resident((GMLP_CHUNK, width), lambda i: (0, 0)),
            _resident((None, width, d), lambda i: (slot, 0, 0)),
            _resident((1, d), lambda i: (0, 0)),
            _resident((1, d), lambda i: (0, 0)),
        ],
        out_specs=pl.BlockSpec((tm, d), lambda i: (i, 0)),
        out_shape=jax.ShapeDtypeStruct((n, d), F32),
        scratch_shapes=[pltpu.VMEM((tm, width), F32)],
        compiler_params=_compiler_params(1),
        name=f"gmlp_s{slot}",
    )(h, win, vg, vb, ws, bs_full, wout, gain, bias)


def _cumsum_rows(x):
    n = x.shape[0]
    row = lax.broadcasted_iota(jnp.int32, x.shape, 0)
    shift = 1
    while shift < n:
        if shift % SUBLANES:
            x = x + jnp.where(row >= shift, pltpu.roll(x, shift, axis=0), 0.0)
        else:
            x = jnp.concatenate([x[:shift], x[shift:] + x[:n - shift]], axis=0)
        shift *= 2
    return x


def _hgrn_projections(xb, win_ref, lb, width):
    q = _silu(_dot(xb, win_ref[:, 0:width]))
    f = lb + (1.0 - lb) * jax.nn.sigmoid(_dot(xb, win_ref[:, width:2 * width]))
    v = _dot(xb, win_ref[:, 2 * width:3 * width])
    gate = _silu(_dot(xb, win_ref[:, 3 * width:4 * width]))
    return q, f, v, gate


def _hgrn_kernel(x_ref, win_ref, lbl_ref, ng_ref, wout_ref, gain_ref, bias_ref, o_ref,
                 state_ref, state0_ref, q_ref, k_ref, v_ref, gc_ref, acc_ref, *, alpha, layer):
    @pl.when(pl.program_id(1) == 0)
    def _():
        state_ref[...] = jnp.zeros_like(state_ref)

    state0_ref[...] = state_ref[...]

    tm = x_ref.shape[0]
    width = wout_ref.shape[0]
    dk = width // HGRN_HEADS
    rg = min(HGRN_ROW_GROUP, tm)
    C = HGRN_CHUNK

    logits = lbl_ref[...]
    e = jnp.exp(logits - jnp.max(logits, axis=0, keepdims=True))
    probs = e / jnp.sum(e, axis=0, keepdims=True)
    lb = jnp.zeros((1, width), F32)
    for j in range(1, layer + 1):
        lb = lb + probs[j:j + 1, :]

    row = lax.broadcasted_iota(jnp.int32, (C, C), 0)
    col = lax.broadcasted_iota(jnp.int32, (C, C), 1)
    lower = col <= row
    norm_g = ng_ref[...]

    def finish_head(h, o_intra, q_d, k_d, v, decay):
        hs = slice(h * dk, (h + 1) * dk)
        st = state_ref[h]
        o_h = o_intra + _dot_nt(q_d[:, hs], st.astype(BF16))
        state_ref[h] = st * decay[:, hs] + _dot_tn(v[:, hs], k_d[:, hs])
        ms = jnp.mean(o_h * o_h, axis=-1, keepdims=True)
        return o_h * lax.rsqrt(ms + RMS_EPS) * norm_g

    def chunk_step(q, f, v, gate):
        gc = _cumsum_rows(jnp.log(f))
        g_last = gc[C - 1:C, :]
        g_mid = 0.5 * g_last
        e_mid = jnp.exp(g_mid)
        q_t = q * jnp.exp(gc - g_mid)
        k_t = (1.0 - f) * jnp.exp(g_mid - gc)
        q_d = (q_t * e_mid).astype(BF16)
        k_d = (k_t * e_mid).astype(BF16)
        q_t = q_t.astype(BF16)
        k_t = k_t.astype(BF16)
        v = v.astype(BF16)
        decay = jnp.exp(g_last)
        outs = []
        for h in range(HGRN_HEADS):
            hs = slice(h * dk, (h + 1) * dk)
            scores = jnp.where(lower, _dot_nt(q_t[:, hs], k_t[:, hs]), 0.0)
            outs.append(finish_head(h, _dot(scores.astype(BF16), v[:, hs]), q_d, k_d, v, decay))
        return (jnp.concatenate(outs, axis=1) * gate).astype(BF16), jnp.abs(g_last)

    worst = jnp.zeros((1, width), F32)
    for r in range(tm // rg):
        rows = slice(r * rg, (r + 1) * rg)
        q, f, v, gate = _hgrn_projections(x_ref[rows, :].astype(BF16), win_ref, lb, width)
        outs = []
        for c in range(rg // C):
            cs = slice(c * C, (c + 1) * C)
            out, g_abs = chunk_step(q[cs], f[cs], v[cs], gate[cs])
            outs.append(out)
            worst = jnp.maximum(worst, g_abs)
        _project_add_norm(o_ref, x_ref, r * rg, jnp.concatenate(outs, axis=0), wout_ref, None,
                          gain_ref, bias_ref, alpha, 1.0)

    @pl.when(jnp.logical_not(jnp.max(worst) <= HGRN_SAFE_LOG_DECAY))
    def _():
        state_ref[...] = state0_ref[...]
        row_w = lax.broadcasted_iota(jnp.int32, (C, width), 0)

        def exact_chunk(c, carry):
            rows = pl.ds(pl.multiple_of(c * C, C), C)
            x = x_ref[rows, :]
            q, f, v, gate = _hgrn_projections(x.astype(BF16), win_ref, lb, width)
            gc = _cumsum_rows(jnp.log(f))
            g_last = gc[C - 1:C, :]
            q_ref[...] = q
            k_ref[...] = 1.0 - f
            v_ref[...] = v
            gc_ref[...] = gc
            acc_ref[...] = jnp.zeros_like(acc_ref)

            def add_key(s, carry):
                sel = pl.ds(s, 1)
                decay = jnp.exp(jnp.minimum(gc_ref[...] - gc_ref[sel, :], 0.0))
                w = jnp.where(row_w >= s, q_ref[...] * k_ref[sel, :] * decay, 0.0)
                v_s = v_ref[sel, :]
                for h in range(HGRN_HEADS):
                    hs = slice(h * dk, (h + 1) * dk)
                    acc_ref[:, hs] += jnp.sum(w[:, hs], axis=-1, keepdims=True) * v_s[:, hs]
                return carry

            lax.fori_loop(0, C, add_key, 0)
            q_d = (q * jnp.exp(gc)).astype(BF16)
            k_d = ((1.0 - f) * jnp.exp(g_last - gc)).astype(BF16)
            vb = v.astype(BF16)
            decay = jnp.exp(g_last)
            outs = [finish_head(h, acc_ref[:, h * dk:(h + 1) * dk], q_d, k_d, vb, decay)
                    for h in range(HGRN_HEADS)]
            out = (jnp.concatenate(outs, axis=1) * gate).astype(BF16)
            mix = _dot(out, wout_ref[...])
            o_ref[rows, :] = _layer_norm(alpha * x + mix, gain_ref[...], bias_ref[...])
            return carry

        lax.fori_loop(0, tm // C, exact_chunk, 0)


def _hgrn_layer(h, batch, win, lb_logits, norm_g, wout, gain, bias, slot, layer, alpha):
    n, d = h.shape
    width = wout.shape[1]
    seq = n // batch
    tm = min(HGRN_TOKEN_TILE, seq)
    spb = seq // tm
    depth = lb_logits.shape[0]
    return pl.pallas_call(
        functools.partial(_hgrn_kernel, alpha=alpha, layer=layer),
        grid=(batch, spb),
        in_specs=[
            pl.BlockSpec((tm, d), lambda b, s: (b * spb + s, 0)),
            _resident((None, d, 4 * width), lambda b, s: (slot, 0, 0)),
            _resident((depth, width), lambda b, s: (0, 0)),
            _resident((1, width // HGRN_HEADS), lambda b, s: (0, 0)),
            _resident((None, width, d), lambda b, s: (slot, 0, 0)),
            _resident((1, d), lambda b, s: (0, 0)),
            _resident((1, d), lambda b, s: (0, 0)),
        ],
        out_specs=pl.BlockSpec((tm, d), lambda b, s: (b * spb + s, 0)),
        out_shape=jax.ShapeDtypeStruct((n, d), F32),
        scratch_shapes=[
            pltpu.VMEM((HGRN_HEADS, width // HGRN_HEADS, width // HGRN_HEADS), F32),
            pltpu.VMEM((HGRN_HEADS, width // HGRN_HEADS, width // HGRN_HEADS), F32),
            pltpu.VMEM((HGRN_CHUNK, width), F32),
            pltpu.VMEM((HGRN_CHUNK, width), F32),
            pltpu.VMEM((HGRN_CHUNK, width), F32),
            pltpu.VMEM((HGRN_CHUNK, width), F32),
            pltpu.VMEM((HGRN_CHUNK, width), F32),
        ],
        compiler_params=_compiler_params(2),
        name=f"hgrn_s{slot}",
    )(h, win, lb_logits, norm_g, wout, gain, bias)


def _swa_kernel(x_ref, wqkv_ref, bqkv_ref, sink_ref, wo_ref, bo_ref, gain_ref, bias_ref, o_ref,
                kv_ref, *, alpha):
    L = ATTN_BLOCK
    hd = ATTN_HEAD_DIM
    group = ATTN_Q_HEADS // ATTN_KV_HEADS
    pairs = group // 2
    kvw = ATTN_KV_HEADS * hd
    dq = ATTN_Q_HEADS * hd
    first_tile = pl.program_id(1) == 0

    @pl.when(first_tile)
    def _():
        kv_ref[0:L, :] = jnp.zeros((L, 2 * kvw), F32)

    tm = x_ref.shape[0]
    rg = min(SWA_ROW_GROUP, tm)

    lane =lax.broadcasted_iota(jnp.int32, (2 * L, kvw), 1)
    low = lane < hd
    qi = lax.broadcasted_iota(jnp.int32, (L, 2 * L), 0)
    kj = lax.broadcasted_iota(jnp.int32, (L, 2 * L), 1)
    in_band = (kj > qi) & (kj <= qi + L)
    ones_cols = jnp.where(low, 1.0, 0.0)
    sink_fill = [jnp.where(kj == 0, sink_ref[h], -jnp.inf) for h in range(ATTN_Q_HEADS)]
    first_key_row = lax.broadcasted_iota(jnp.int32, (2 * L, kvw), 0) == 0

    def attend(qb, kv2, visible):
        k2 = kv2[:, 0:kvw]
        v2 = jnp.where(first_key_row, 0.0, kv2[:, kvw:2 * kvw])
        k2r = pltpu.roll(k2, hd, axis=1)
        v2r = pltpu.roll(v2, hd, axis=1)
        blocks = []
        for hk in range(ATTN_KV_HEADS):
            k_lo, k_hi = (k2, k2r) if hk == 0 else (k2r, k2)
            v_lo, v_hi = (v2, v2r) if hk == 0 else (v2r, v2)
            keys = jnp.concatenate([jnp.where(low, k_lo, 0.0), jnp.where(low, 0.0, k_hi)],
                                   axis=0).astype(BF16)
            vals = jnp.concatenate(
                [jnp.concatenate([jnp.where(low, v_lo, 0.0), ones_cols], axis=1),
                 jnp.concatenate([jnp.where(low, 0.0, v_hi), 1.0 - ones_cols], axis=1)],
                axis=0).astype(BF16)
            q_pairs = jnp.concatenate(
                [qb[:, (hk * pairs + p) * 2 * hd:(hk * pairs + p + 1) * 2 * hd] for p in range(pairs)],
                axis=0)
            s_all = _dot_nt(q_pairs, keys)
            probs = []
            for p in range(pairs):
                head = hk * group + 2 * p
                s_e = jnp.where(visible, s_all[p * L:(p + 1) * L, 0:2 * L], sink_fill[head])
                s_o = jnp.where(visible, s_all[p * L:(p + 1) * L, 2 * L:4 * L], sink_fill[head + 1])
                m_e = jnp.max(s_e, axis=-1, keepdims=True)
                m_o = jnp.max(s_o, axis=-1, keepdims=True)
                probs.append(jnp.concatenate([jnp.exp(s_e - m_e), jnp.exp(s_o - m_o)], axis=1).astype(BF16))
            pv = _dot(jnp.concatenate(probs, axis=0), vals)
            out = pv[:, 0:2 * hd] / pv[:, 2 * hd:4 * hd]
            blocks.extend(out[p * L:(p + 1) * L, :] for p in range(pairs))
        return jnp.concatenate(blocks, axis=1).astype(BF16)

    for r in range(tm // rg):
        rows = slice(r * rg, (r + 1) * rg)
        xb = x_ref[rows, :].astype(BF16)
        q = ((_dot(xb, wqkv_ref[:, 0:dq]) + bqkv_ref[:, 0:dq]) * (hd ** -0.5)).astype(BF16)
        kv_ref[L + r * rg:L + (r + 1) * rg, :] = (
            _dot(xb, wqkv_ref[:, dq:dq + 2 * kvw]) + bqkv_ref[:, dq:dq + 2 * kvw])
        outs = []
        for nb in range(rg // L):
            n = r * (rg // L) + nb
            visible = in_band & (kj >= jnp.where(first_tile, L, 0)) if n == 0 else in_band
            outs.append(attend(q[nb * L:(nb + 1) * L, :], kv_ref[n * L:(n + 2) * L, :], visible))
        _project_add_norm(o_ref, x_ref, r * rg, jnp.concatenate(outs, axis=0), wo_ref, bo_ref[...],
                          gain_ref, bias_ref, alpha, 1.0)

    kv_ref[0:L, :] = kv_ref[tm:tm + L, :]


def _swa_layer(h, batch, wqkv, bqkv, sinks, wo, bo, gain, bias, slot, alpha):
    n, d = h.shape
    seq = n // batch
    tm = min(SWA_TOKEN_TILE, seq)
    spb = seq // tm
    qkv_w = wqkv.shape[2]
    dq = wo.shape[1]
    kvw = ATTN_KV_HEADS * ATTN_HEAD_DIM
    return pl.pallas_call(
        functools.partial(_swa_kernel, alpha=alpha),
        grid=(batch, spb),
        in_specs=[
            pl.BlockSpec((tm, d), lambda b, s: (b * spb + s, 0)),
            _resident((None, d, qkv_w), lambda b, s: (slot, 0, 0)),
            _resident((1, qkv_w), lambda b, s: (0, 0)),
            pl.BlockSpec(memory_space=pltpu.SMEM),
            _resident((None, dq, d), lambda b, s: (slot, 0, 0)),
            _resident((1, d), lambda b, s: (0, 0)),
            _resident((1, d), lambda b, s: (0, 0)),
            _resident((1, d), lambda b, s: (0, 0)),
        ],
        out_specs=pl.BlockSpec((tm, d), lambda b, s: (b * spb + s, 0)),
        out_shape=jax.ShapeDtypeStruct((n, d), F32),
        scratch_shapes=[
            pltpu.VMEM((ATTN_BLOCK + tm, 2 * kvw), F32),
        ],
        compiler_params=_compiler_params(2),
        name=f"swa_s{slot}",
    )(h, wqkv, bqkv, sinks, wo, bo, gain, bias)


def kernel(x, ffn_w_gate_up, ffn_w_down, ln_gain, ln_bias, gmlp_w_in, gmlp_ln_gain, gmlp_ln_bias, gmlp_w_spatial, gmlp_b_spatial, gmlp_w_out, hgrn_w_in, hgrn_lb_logits, hgrn_norm_gain, hgrn_w_out, attn_w_qkv, attn_b_qkv, attn_sinks, attn_w_o, attn_b_o):
    batch, seq, d = x.shape
    depth = ffn_w_gate_up.shape[0]
    n_mixers = 3
    alpha = (2.0 * depth) ** 0.25

    wgu = ffn_w_gate_up
    wd = ffn_w_down
    gmlp_win = gmlp_w_in.astype(BF16)
    gmlp_wout = gmlp_w_out.astype(BF16)
    hgrn_win = hgrn_w_in.astype(BF16)
    hgrn_wout = hgrn_w_out.astype(BF16)
    attn_wqkv = attn_w_qkv.astype(BF16)
    attn_wo = attn_w_o.astype(BF16)

    h = x.reshape(batch * seq, d)
    for layer in range(depth):
        kind = layer % n_mixers
        slot = layer // n_mixers
        h = _ffn_layer(h, wgu, wd, ln_gain[layer, 0][None], ln_bias[layer, 0][None], layer, 0, alpha)
        gain, bias = ln_gain[layer, 1][None], ln_bias[layer, 1][None]
        if kind == 0:
            group_dim = gmlp_w_out.shape[1] // GMLP_GROUPS
            bs_full = jnp.repeat(jnp.transpose(gmlp_b_spatial[slot]), group_dim, axis=1)
            h = _gmlp_layer(h, gmlp_win, gmlp_ln_gain[slot][None], gmlp_ln_bias[slot][None],
                            gmlp_w_spatial, bs_full, gmlp_wout, gain, bias, slot, alpha)
        elif kind == 1:
            h = _hgrn_layer(h, batch, hgrn_win, hgrn_lb_logits, hgrn_norm_gain[slot][None],
                            hgrn_wout, gain, bias, slot, layer, alpha)
        else:
            h = _swa_layer(h, batch, attn_wqkv, attn_b_qkv[slot][None], attn_sinks[slot],
                           attn_wo, attn_b_o[slot][None], gain, bias, slot, alpha)
        h = _ffn_layer(h, wgu, wd, ln_gain[layer, 2][None], ln_bias[layer, 2][None], layer, 1, alpha)
    return h.reshape(batch, seq, d)
```

```python
import functools
import math

import jax
import jax.numpy as jnp
from jax import lax
from jax.experimental import pallas as pl
from jax.experimental.pallas import tpu as pltpu

F32 = jnp.float32
BF16 = jnp.bfloat16

LN_EPS = 1e-5
RMS_EPS = 1e-6

SUBLANES = 8
VMEM_LIMIT_BYTES = 56 * 1024 * 1024

FFN_TOKEN_TILE = 1024
OUT_ROW_GROUP = 256
FFN_COL_TILE = 256
WEIGHT_CHUNKS = 8
GMLP_TOKEN_TILE = 1024
GMLP_ROW_GROUP = 1024
HGRN_TOKEN_TILE = 512
HGRN_ROW_GROUP = 256
SWA_TOKEN_TILE = 512
SWA_ROW_GROUP = 512

GMLP_CHUNK = 128
GMLP_GROUPS = 8
HGRN_HEADS = 8
HGRN_CHUNK = 128
HGRN_SAFE_LOG_DECAY = 150.0
ATTN_BLOCK = 128
ATTN_HEAD_DIM = 64
ATTN_Q_HEADS = 16
ATTN_KV_HEADS = 2


def _dot(a, b):
    return jnp.dot(a, b, preferred_element_type=F32)


def _dot_nt(a, b):
    return lax.dot_general(a, b, (((1,), (1,)), ((), ())), preferred_element_type=F32)


def _dot_tn(a, b):
    return lax.dot_general(a, b, (((0,), (0,)), ((), ())), preferred_element_type=F32)


def _layer_norm(y, gain, bias):
    mu = jnp.mean(y, axis=-1, keepdims=True)
    yc = y - mu
    var = jnp.mean(yc * yc, axis=-1, keepdims=True)
    return yc * lax.rsqrt(var + LN_EPS) * gain + bias


def _silu(x):
    return x * jax.nn.sigmoid(x)


def _gelu_erf(x):
    return 0.5 * x * (1.0 + lax.erf(x * math.sqrt(0.5)))


def _project_add_norm(o_ref, x_ref, row0, act, w_ref, b_out, gain_ref, bias_ref, alpha, scale):
    n = act.shape[0]
    rg = min(OUT_ROW_GROUP, n)
    for r in range(n // rg):
        mix = _dot(act[r * rg:(r + 1) * rg, :], w_ref[...])
        if b_out is not None:
            mix = mix + b_out
        if scale != 1.0:
            mix = scale * mix
        rows = slice(row0 + r * rg, row0 + (r + 1) * rg)
        o_ref[rows, :] = _layer_norm(alpha * x_ref[rows, :] + mix, gain_ref[...], bias_ref[...])


def _resident(block_shape, index_map):
    return pl.BlockSpec(block_shape, index_map, pipeline_mode=pl.Buffered(1))


def _compiler_params(n_grid_axes):
    return pltpu.CompilerParams(
        dimension_semantics=("arbitrary",) * n_grid_axes,
        vmem_limit_bytes=VMEM_LIMIT_BYTES,
    )


def _weight_chunk_copy(w_hbm, stage_ref, sem_ref, c):
    rows = stage_ref.shape[1]
    return pltpu.make_async_copy(w_hbm.at[pl.ds(c * rows, rows), :], stage_ref.at[c % 2], sem_ref.at[c % 2])


def _load_weights_as_bf16(streams):
    n_chunks = {w_ref.shape[0] // stage_ref.shape[1] for _, w_ref, stage_ref, _ in streams}
    (n_chunks,) = n_chunks
    for w_hbm, _, stage_ref, sem_ref in streams:
        _weight_chunk_copy(w_hbm, stage_ref, sem_ref, 0).start()
    for c in range(n_chunks):
        for w_hbm, w_ref, stage_ref, sem_ref in streams:
            rows = stage_ref.shape[1]
            if c + 1 < n_chunks:
                _weight_chunk_copy(w_hbm, stage_ref, sem_ref, c + 1).start()
            _weight_chunk_copy(w_hbm, stage_ref, sem_ref, c).wait()
            w_ref[c * rows:(c + 1) * rows, :] = stage_ref[c % 2].astype(BF16)


def _weight_scratch(*shapes):
    copies = [pltpu.VMEM(shape, BF16) for shape in shapes]
    stages = [pltpu.VMEM((2, rows // WEIGHT_CHUNKS, cols), F32) for rows, cols in shapes]
    sems = [pltpu.SemaphoreType.DMA((2,)) for _ in shapes]
    return copies + stages + sems


def _ffn_kernel(x_ref, wgu_hbm, wd_hbm, gain_ref, bias_ref, o_ref,
                wgu_ref, wd_ref, act_ref, stage_gu_ref, stage_d_ref, sem_gu_ref, sem_d_ref,
                *, alpha, layer, half):
    @pl.when(pl.program_id(0) == 0)
    def _():
        _load_weights_as_bf16([(wgu_hbm.at[layer, half], wgu_ref, stage_gu_ref, sem_gu_ref),
                               (wd_hbm.at[layer, half], wd_ref, stage_d_ref, sem_d_ref)])

    xb = x_ref[...].astype(BF16)
    d_ff = wd_ref.shape[0]
    for c in range(d_ff // FFN_COL_TILE):
        lo = c * FFN_COL_TILE
        gate = _dot(xb, wgu_ref[:, lo:lo + FFN_COL_TILE])
        up = _dot(xb, wgu_ref[:, d_ff + lo:d_ff + lo + FFN_COL_TILE])
        act_ref[:, lo:lo + FFN_COL_TILE] = (_silu(gate) * up).astype(BF16)
    _project_add_norm(o_ref, x_ref, 0, act_ref, wd_ref, None, gain_ref, bias_ref, alpha, 0.5)


def _ffn_layer(h, wgu, wd, gain, bias, layer, half, alpha):
    n, d = h.shape
    d_ff = wd.shape[2]
    tm = min(FFN_TOKEN_TILE, n)
    return pl.pallas_call(
        functools.partial(_ffn_kernel, alpha=alpha, layer=layer, half=half),
        grid=(n // tm,),
        in_specs=[
            pl.BlockSpec((tm, d), lambda i: (i, 0)),
            pl.BlockSpec(memory_space=pl.ANY),
            pl.BlockSpec(memory_space=pl.ANY),
            _resident((1, d), lambda i: (0, 0)),
            _resident((1, d), lambda i: (0, 0)),
        ],
        out_specs=pl.BlockSpec((tm, d), lambda i: (i, 0)),
        out_shape=jax.ShapeDtypeStruct((n, d), F32),
        scratch_shapes=[
            pltpu.VMEM((d, 2 * d_ff), BF16),
            pltpu.VMEM((d_ff, d), BF16),
            pltpu.VMEM((tm, d_ff), BF16),
            pltpu.VMEM((2, d // WEIGHT_CHUNKS, 2 * d_ff), F32),
            pltpu.VMEM((2, d_ff // WEIGHT_CHUNKS, d), F32),
            pltpu.SemaphoreType.DMA((2,)),
            pltpu.SemaphoreType.DMA((2,)),
        ],
        compiler_params=_compiler_params(1),
        name=f"ffn_l{layer}_h{half}",
    )(h, wgu, wd, gain, bias)


def _gmlp_kernel(x_ref, win_hbm, vg_ref, vb_ref, ws_ref, bs_ref, wout_hbm, gain_ref, bias_ref,
                 o_ref, gated_ref, win_ref, wout_ref, stage_in_ref, stage_out_ref, sem_in_ref, sem_out_ref,
                 *, alpha, slot):
    @pl.when(pl.program_id(0) == 0)
    def _():
        _load_weights_as_bf16([(win_hbm.at[slot], win_ref, stage_in_ref, sem_in_ref),
                               (wout_hbm.at[slot], wout_ref, stage_out_ref, sem_out_ref)])

    tm = x_ref.shape[0]
    width = wout_ref.shape[0]
    gd = width // GMLP_GROUPS
    rg = min(GMLP_ROW_GROUP, tm)
    n_chunks = rg // GMLP_CHUNK

    t_idx = lax.broadcasted_iota(jnp.int32, (GMLP_CHUNK, GMLP_CHUNK), 0)
    s_idx = lax.broadcasted_iota(jnp.int32, (GMLP_CHUNK, GMLP_CHUNK), 1)
    causal = s_idx <= t_idx
    w_s = [jnp.where(causal, ws_ref[g], 0.0).astype(BF16) for g in range(GMLP_GROUPS)]

    for r in range(tm // rg):
        rows = slice(r * rg, (r + 1) * rg)
        xb = x_ref[rows, :].astype(BF16)
        v = _gelu_erf(_dot(xb, win_ref[:, width:]))
        u = _gelu_erf(_dot(xb, win_ref[:, :width]))
        v = _layer_norm(v, vg_ref[...], vb_ref[...]).astype(BF16)
        for g in range(GMLP_GROUPS):
            v_g = jnp.concatenate(
                [v[c * GMLP_CHUNK:(c + 1) * GMLP_CHUNK, g * gd:(g + 1) * gd] for c in range(n_chunks)],
                axis=1)
            mixed = _dot(w_s[g], v_g)
            b_g = bs_ref[:, g * gd:(g + 1) * gd]
            for c in range(n_chunks):
                gated_ref[r * rg + c * GMLP_CHUNK:r * rg + (c + 1) * GMLP_CHUNK, g * gd:(g + 1) * gd] = (
                    mixed[:, c * gd:(c + 1) * gd] + b_g)
        _project_add_norm(o_ref, x_ref, r * rg, (u * gated_ref[rows, :]).astype(BF16), wout_ref, None,
                          gain_ref, bias_ref, alpha, 1.0)


def _gmlp_layer(h, win, vg, vb, ws, bs_full, wout, gain, bias, slot, alpha):
    n, d = h.shape
    width = wout.shape[1]
    tm = min(GMLP_TOKEN_TILE, n)
    return pl.pallas_call(
        functools.partial(_gmlp_kernel, alpha=alpha, slot=slot),
        grid=(n // tm,),
        in_specs=[
            pl.BlockSpec((tm, d), lambda i: (i, 0)),
            pl.BlockSpec(memory_space=pl.ANY),
            _resident((1, width), lambda i: (0, 0)),
            _resident((1, width), lambda i: (0, 0)),
            _resident((None, GMLP_GROUPS, GMLP_CHUNK, GMLP_CHUNK), lambda i: (slot, 0, 0, 0)),
            _resident((GMLP_CHUNK, width), lambda i: (0, 0)),
            pl.BlockSpec(memory_space=pl.ANY),
            _resident((1, d), lambda i: (0, 0)),
            _resident((1, d), lambda i: (0, 0)),
        ],
        out_specs=pl.BlockSpec((tm, d), lambda i: (i, 0)),
        out_shape=jax.ShapeDtypeStruct((n, d), F32),
        scratch_shapes=[pltpu.VMEM((tm, width), F32)] + _weight_scratch((d, 2 * width), (width, d)),
        compiler_params=_compiler_params(1),
        name=f"gmlp_s{slot}",
    )(h, win, vg, vb, ws, bs_full, wout, gain, bias)


def _cumsum_rows(x):
    n = x.shape[0]
    row = lax.broadcasted_iota(jnp.int32, x.shape, 0)
    shift = 1
    while shift < n:
        if shift % SUBLANES:
            x = x + jnp.where(row >= shift, pltpu.roll(x, shift, axis=0), 0.0)
        else:
            x = jnp.concatenate([x[:shift], x[shift:] + x[:n - shift]], axis=0)
        shift *= 2
    return x


def _hgrn_projections(xb, win_ref, lb, width):
    q = _silu(_dot(xb, win_ref[:, 0:width]))
    f = lb + (1.0 - lb) * jax.nn.sigmoid(_dot(xb, win_ref[:, width:2 * width]))
    v = _dot(xb, win_ref[:, 2 * width:3 * width])
    gate = _silu(_dot(xb, win_ref[:, 3 * width:4 * width]))
    return q, f, v, gate


def _hgrn_kernel(x_ref, win_hbm, lbl_ref, ng_ref, wout_hbm, gain_ref, bias_ref, o_ref,
                 state_ref, state0_ref, q_ref, k_ref, v_ref, gc_ref, acc_ref,
                 win_ref, wout_ref, stage_in_ref, stage_out_ref, sem_in_ref, sem_out_ref,
                 *, alpha, layer, slot):
    @pl.when((pl.program_id(0) == 0) & (pl.program_id(1) == 0))
    def _():
        _load_weights_as_bf16([(win_hbm.at[slot], win_ref, stage_in_ref, sem_in_ref),
                               (wout_hbm.at[slot], wout_ref, stage_out_ref, sem_out_ref)])

    @pl.when(pl.program_id(1) == 0)
    def _():
        state_ref[...] = jnp.zeros_like(state_ref)

    state0_ref[...] = state_ref[...]

    tm = x_ref.shape[0]
    width = wout_ref.shape[0]
    dk = width // HGRN_HEADS
    rg = min(HGRN_ROW_GROUP, tm)
    C = HGRN_CHUNK

    logits = lbl_ref[...]
    e = jnp.exp(logits - jnp.max(logits, axis=0, keepdims=True))
    probs = e / jnp.sum(e, axis=0, keepdims=True)
    lb = jnp.zeros((1, width), F32)
    for j in range(1, layer + 1):
        lb = lb + probs[j:j + 1, :]

    row = lax.broadcasted_iota(jnp.int32, (C, C), 0)
    col = lax.broadcasted_iota(jnp.int32, (C, C), 1)
    lower = col <= row
    norm_g = ng_ref[...]

    def finish_head(h, o_intra, q_d, k_d, v, decay):
        hs = slice(h * dk, (h + 1) * dk)
        st = state_ref[h]
        o_h = o_intra + _dot_nt(q_d[:, hs], st.astype(BF16))
        state_ref[h] = st * decay[:, hs] + _dot_tn(v[:, hs], k_d[:, hs])
        ms = jnp.mean(o_h * o_h, axis=-1, keepdims=True)
        return o_h * lax.rsqrt(ms + RMS_EPS) * norm_g

    def chunk_step(q, f, v, gate):
        gc = _cumsum_rows(jnp.log(f))
        g_last = gc[C - 1:C, :]
        g_mid = 0.5 * g_last
        e_mid = jnp.exp(g_mid)
        q_t = q * jnp.exp(gc - g_mid)
        k_t = (1.0 - f) * jnp.exp(g_mid - gc)
        q_d = (q_t * e_mid).astype(BF16)
        k_d = (k_t * e_mid).astype(BF16)
        q_t = q_t.astype(BF16)
        k_t = k_t.astype(BF16)
        v = v.astype(BF16)
        decay = jnp.exp(g_last)
        outs = []
        for h in range(HGRN_HEADS):
            hs = slice(h * dk, (h + 1) * dk)
            scores = jnp.where(lower, _dot_nt(q_t[:, hs], k_t[:, hs]), 0.0)
            outs.append(finish_head(h, _dot(scores.astype(BF16), v[:, hs]), q_d, k_d, v, decay))
        return (jnp.concatenate(outs, axis=1) * gate).astype(BF16), jnp.abs(g_last)

    worst = jnp.zeros((1, width), F32)
    for r in range(tm // rg):
        rows = slice(r * rg, (r + 1) * rg)
        q, f, v, gate = _hgrn_projections(x_ref[rows, :].astype(BF16), win_ref, lb, width)
        outs = []
        for c in range(rg // C):
            cs = slice(c * C, (c + 1) * C)
            out, g_abs = chunk_step(q[cs], f[cs], v[cs], gate[cs])
            outs.append(out)
            worst = jnp.maximum(worst, g_abs)
        _project_add_norm(o_ref, x_ref, r * rg, jnp.concatenate(outs, axis=0), wout_ref, None,
                          gain_ref, bias_ref, alpha, 1.0)

    @pl.when(jnp.logical_not(jnp.max(worst) <= HGRN_SAFE_LOG_DECAY))
    def _():
        state_ref[...] = state0_ref[...]
        row_w = lax.broadcasted_iota(jnp.int32, (C, width), 0)

        def exact_chunk(c, carry):
            rows = pl.ds(pl.multiple_of(c * C, C), C)
            x = x_ref[rows, :]
            q, f, v, gate = _hgrn_projections(x.astype(BF16), win_ref, lb, width)
            gc = _cumsum_rows(jnp.log(f))
            g_last = gc[C - 1:C, :]
            q_ref[...] = q
            k_ref[...] = 1.0 - f
            v_ref[...] = v
            gc_ref[...] = gc
            acc_ref[...] = jnp.zeros_like(acc_ref)

            def add_key(s, carry):
                sel = pl.ds(s, 1)
                decay = jnp.exp(jnp.minimum(gc_ref[...] - gc_ref[sel, :], 0.0))
                w = jnp.where(row_w >= s, q_ref[...] * k_ref[sel, :] * decay, 0.0)
                v_s = v_ref[sel, :]
                for h in range(HGRN_HEADS):
                    hs = slice(h * dk, (h + 1) * dk)
                    acc_ref[:, hs] += jnp.sum(w[:, hs], axis=-1, keepdims=True) * v_s[:, hs]
                return carry

            lax.fori_loop(0, C, add_key, 0)
            q_d = (q * jnp.exp(gc)).astype(BF16)
            k_d = ((1.0 - f) * jnp.exp(g_last - gc)).astype(BF16)
            vb = v.astype(BF16)
            decay = jnp.exp(g_last)
            outs = [finish_head(h, acc_ref[:, h * dk:(h + 1) * dk], q_d, k_d, vb, decay)
                    for h in range(HGRN_HEADS)]
            out = (jnp.concatenate(outs, axis=1) * gate).astype(BF16)
            mix = _dot(out, wout_ref[...])
            o_ref[rows, :] = _layer_norm(alpha * x + mix, gain_ref[...], bias_ref[...])
            return carry

        lax.fori_loop(0, tm // C, exact_chunk, 0)


def _hgrn_layer(h, batch, win, lb_logits, norm_g, wout, gain, bias, slot, layer, alpha):
    n, d = h.shape
    width = wout.shape[1]
    seq = n // batch
    tm = min(HGRN_TOKEN_TILE, seq)
    spb = seq // tm
    depth = lb_logits.shape[0]
    return pl.pallas_call(
        functools.partial(_hgrn_kernel, alpha=alpha, layer=layer, slot=slot),
        grid=(batch, spb),
        in_specs=[
            pl.BlockSpec((tm, d), lambda b, s: (b * spb + s, 0)),
            pl.BlockSpec(memory_space=pl.ANY),
            _resident((depth, width), lambda b, s: (0, 0)),
            _resident((1, width // HGRN_HEADS), lambda b, s: (0, 0)),
            pl.BlockSpec(memory_space=pl.ANY),
            _resident((1, d), lambda b, s: (0, 0)),
            _resident((1, d), lambda b, s: (0, 0)),
        ],
        out_specs=pl.BlockSpec((tm, d), lambda b, s: (b * spb + s, 0)),
        out_shape=jax.ShapeDtypeStruct((n, d), F32),
        scratch_shapes=[
            pltpu.VMEM((HGRN_HEADS, width // HGRN_HEADS, width // HGRN_HEADS), F32),
            pltpu.VMEM((HGRN_HEADS, width // HGRN_HEADS, width // HGRN_HEADS), F32),
            pltpu.VMEM((HGRN_CHUNK, width), F32),
            pltpu.VMEM((HGRN_CHUNK, width), F32),
            pltpu.VMEM((HGRN_CHUNK, width), F32),
            pltpu.VMEM((HGRN_CHUNK, width), F32),
            pltpu.VMEM((HGRN_CHUNK, width), F32),
        ] + _weight_scratch((d, 4 * width), (width, d)),
        compiler_params=_compiler_params(2),
        name=f"hgrn_s{slot}",
    )(h, win, lb_logits, norm_g, wout, gain, bias)


def _swa_kernel(x_ref, wqkv_hbm, bqkv_ref, sink_ref, wo_hbm, bo_ref, gain_ref, bias_ref, o_ref,
                kv_ref, wqkv_ref, wo_ref, stage_in_ref, stage_out_ref, sem_in_ref, sem_out_ref,
                *, alpha, slot):
    @pl.when((pl.program_id(0) == 0) & (pl.program_id(1) == 0))
    def _():
        _load_weights_as_bf16([(wqkv_hbm.at[slot], wqkv_ref, stage_in_ref, sem_in_ref),
                               (wo_hbm.at[slot], wo_ref, stage_out_ref, sem_out_ref)])

    L = ATTN_BLOCK
    hd = ATTN_HEAD_DIM
    group = ATTN_Q_HEADS // ATTN_KV_HEADS
    pairs = group // 2
    kvw = ATTN_KV_HEADS * hd
    dq = ATTN_Q_HEADS * hd
    first_tile = pl.program_id(1) == 0

    @pl.when(first_tile)
    def _():
        kv_ref[0:L, :] = jnp.zeros((L, 2 * kvw), F32)

    tm = x_ref.shape[0]
    rg = min(SWA_ROW_GROUP, tm)

    lane = lax.broadcasted_iota(jnp.int32, (2 * L, kvw), 1)
    low = lane < hd
    qi = lax.broadcasted_iota(jnp.int32, (L, 2 * L), 0)
    kj = lax.broadcasted_iota(jnp.int32, (L, 2 * L), 1)
    in_band = (kj > qi) & (kj <= qi + L)
    ones_cols = jnp.where(low, 1.0, 0.0)
    sink_fill = [jnp.where(kj == 0, sink_ref[h], -jnp.inf) for h in range(ATTN_Q_HEADS)]
    first_key_row = lax.broadcasted_iota(jnp.int32, (2 * L, kvw), 0) == 0

    def attend(qb, kv2, visible):
        k2 = kv2[:, 0:kvw]
        v2 = jnp.where(first_key_row, 0.0, kv2[:, kvw:2 * kvw])
        k2r = pltpu.roll(k2, hd, axis=1)
        v2r = pltpu.roll(v2, hd, axis=1)
        blocks = []
        for hk in range(ATTN_KV_HEADS):
            k_lo, k_hi = (k2, k2r) if hk == 0 else (k2r, k2)
            v_lo, v_hi = (v2, v2r) if hk == 0 else (v2r, v2)
            keys = jnp.concatenate([jnp.where(low, k_lo, 0.0), jnp.where(low, 0.0, k_hi)],
                                   axis=0).astype(BF16)
            vals = jnp.concatenate(
                [jnp.concatenate([jnp.where(low, v_lo, 0.0), ones_cols], axis=1),
                 jnp.concatenate([jnp.where(low, 0.0, v_hi), 1.0 - ones_cols], axis=1)],
                axis=0).astype(BF16)
            q_pairs = jnp.concatenate(
                [qb[:, (hk * pairs + p) * 2 * hd:(hk * pairs + p + 1) * 2 * hd] for p in range(pairs)],
                axis=0)
            s_all = _dot_nt(q_pairs, keys)
            probs = []
            for p in range(pairs):
                head = hk * group + 2 * p
                s_e = jnp.where(visible, s_all[p * L:(p + 1) * L, 0:2 * L], sink_fill[head])
                s_o = jnp.where(visible, s_all[p * L:(p + 1) * L, 2 * L:4 * L], sink_fill[head + 1])
                m_e = jnp.max(s_e, axis=-1, keepdims=True)
                m_o = jnp.max(s_o, axis=-1, keepdims=True)
                probs.append(jnp.concatenate([jnp.exp(s_e - m_e), jnp.exp(s_o - m_o)], axis=1).astype(BF16))
            pv = _dot(jnp.concatenate(probs, axis=0), vals)
            out = pv[:, 0:2 * hd] / pv[:, 2 * hd:4 * hd]
            blocks.extend(out[p * L:(p + 1) * L, :] for p in range(pairs))
        return jnp.concatenate(blocks, axis=1).astype(BF16)

    for r in range(tm // rg):
        rows = slice(r * rg, (r + 1) * rg)
        xb = x_ref[rows, :].astype(BF16)
        q = ((_dot(xb, wqkv_ref[:, 0:dq]) + bqkv_ref[:, 0:dq]) * (hd ** -0.5)).astype(BF16)
        kv_ref[L + r * rg:L + (r + 1) * rg, :] = (
            _dot(xb, wqkv_ref[:, dq:dq + 2 * kvw]) + bqkv_ref[:, dq:dq + 2 * kvw])
        outs = []
        for nb in range(rg // L):
            n = r * (rg // L) + nb
            visible = in_band & (kj >= jnp.where(first_tile, L, 0)) if n == 0 else in_band
            outs.append(attend(q[nb * L:(nb + 1) * L, :], kv_ref[n * L:(n + 2) * L, :], visible))
        _project_add_norm(o_ref, x_ref, r * rg, jnp.concatenate(outs, axis=0), wo_ref, bo_ref[...],
                          gain_ref, bias_ref, alpha, 1.0)

    kv_ref[0:L, :] = kv_ref[tm:tm + L, :]


def _swa_layer(h, batch, wqkv, bqkv, sinks, wo, bo, gain, bias, slot, alpha):
    n, d = h.shape
    seq = n // batch
    tm = min(SWA_TOKEN_TILE, seq)
    spb = seq // tm
    qkv_w = wqkv.shape[2]
    dq = wo.shape[1]
    kvw = ATTN_KV_HEADS * ATTN_HEAD_DIM
    return pl.pallas_call(
        functools.partial(_swa_kernel, alpha=alpha, slot=slot),
        grid=(batch, spb),
        in_specs=[
            pl.BlockSpec((tm, d), lambda b, s: (b * spb + s, 0)),
            pl.BlockSpec(memory_space=pl.ANY),
            _resident((1, qkv_w), lambda b, s: (0, 0)),
            pl.BlockSpec(memory_space=pltpu.SMEM),
            pl.BlockSpec(memory_space=pl.ANY),
            _resident((1, d), lambda b, s: (0, 0)),
            _resident((1, d), lambda b, s: (0, 0)),
            _resident((1, d), lambda b, s: (0, 0)),
        ],
        out_specs=pl.BlockSpec((tm, d), lambda b, s: (b * spb + s, 0)),
        out_shape=jax.ShapeDtypeStruct((n, d), F32),
        scratch_shapes=[
            pltpu.VMEM((ATTN_BLOCK + tm, 2 * kvw), F32),
        ] + _weight_scratch((d, qkv_w), (dq, d)),
        compiler_params=_compiler_params(2),
        name=f"swa_s{slot}",
    )(h, wqkv, bqkv, sinks, wo, bo, gain, bias)


def kernel(x, ffn_w_gate_up, ffn_w_down, ln_gain, ln_bias, gmlp_w_in, gmlp_ln_gain, gmlp_ln_bias, gmlp_w_spatial, gmlp_b_spatial, gmlp_w_out, hgrn_w_in, hgrn_lb_logits, hgrn_norm_gain, hgrn_w_out, attn_w_qkv, attn_b_qkv, attn_sinks, attn_w_o, attn_b_o):
    batch, seq, d = x.shape
    depth = ffn_w_gate_up.shape[0]
    n_mixers = 3
    alpha = (2.0 * depth) ** 0.25

    h = x.reshape(batch * seq, d)
    for layer in range(depth):
        kind = layer % n_mixers
        slot = layer // n_mixers
        h = _ffn_layer(h, ffn_w_gate_up, ffn_w_down, ln_gain[layer, 0][None], ln_bias[layer, 0][None], layer, 0, alpha)
        gain, bias = ln_gain[layer, 1][None], ln_bias[layer, 1][None]
        if kind == 0:
            group_dim = gmlp_w_out.shape[1] // GMLP_GROUPS
            bs_full = jnp.repeat(jnp.transpose(gmlp_b_spatial[slot]), group_dim, axis=1)
            h = _gmlp_layer(h, gmlp_w_in, gmlp_ln_gain[slot][None], gmlp_ln_bias[slot][None],
                            gmlp_w_spatial, bs_full, gmlp_w_out, gain, bias, slot, alpha)
        elif kind == 1:
            h = _hgrn_layer(h, batch, hgrn_w_in, hgrn_lb_logits, hgrn_norm_gain[slot][None],
                            hgrn_w_out, gain, bias, slot, layer, alpha)
        else:
            h = _swa_layer(h, batch, attn_w_qkv, attn_b_qkv[slot][None], attn_sinks[slot],
                           attn_w_o, attn_b_o[slot][None], gain, bias, slot, alpha)
        h = _ffn_layer(h, ffn_w_gate_up, ffn_w_down, ln_gain[layer, 2][None], ln_bias[layer, 2][None], layer, 1, alpha)
    return h.reshape(batch, seq, d)
```
